```python
import functools
import math
import jax
import jax.numpy as jnp
from jax import lax
import numpy as np

D_MODEL = 1024
BATCH = 2
SEQ = 8192
DEPTH = 2

GRID_W = 64
CTX_LEN = 256
ROPE_THETA = 10000.0
NORM_EPS = 1e-6
NEG_INF = -1e30
BLOCK = 128
WINDOW = 128

A_HEADS = 8
A_KV_HEADS = 2
A_HEAD_DIM = 64
B_HEADS = 4
B_QK_DIM = 64
B_V_DIM = 2 * B_QK_DIM
A_Q_W = A_HEADS * A_HEAD_DIM
A_KV_W = A_KV_HEADS * A_HEAD_DIM
B_QK_W = B_HEADS * 2 * B_QK_DIM
B_V_W = B_HEADS * B_V_DIM
AB_Q_W = A_Q_W + B_QK_W
AB_IN_W = AB_Q_W + 2 * A_KV_W + B_QK_W + B_V_W
AB_OUT_W = A_Q_W + B_V_W

C_HEADS = 16
C_Q_LORA = 384
C_KV_LORA = 256
C_NOPE = 64
C_ROPE = 32
C_V = 64
C_IN_W = C_Q_LORA + C_KV_LORA + C_ROPE
C_OUT_W = C_HEADS * C_V

FFN_HIDDEN = ((8 * D_MODEL + 3 * 256 - 1) // (3 * 256)) * 256

N_EVEN = (DEPTH + 1) // 2
N_ODD = DEPTH // 2

F32 = jnp.float32

kernel_name = 'hybrid_dit_swa_diff_mla_prefix'


def rmsnorm(x, g):
    xf = x.astype(F32)
    y = xf * lax.rsqrt(jnp.mean(xf * xf, axis=-1, keepdims=True) + NORM_EPS)
    return (y * g.astype(F32)).astype(x.dtype)


def modulate(xn, shift, scale):
    return xn * (1.0 + scale) + shift


def joint_softmax(*logits):
    m = functools.reduce(jnp.maximum, [s.max(axis=-1, keepdims=True) for s in logits])
    e = [jnp.exp(s - m) for s in logits]
    inv = 1.0 / functools.reduce(jnp.add, [t.sum(axis=-1, keepdims=True) for t in e])
    return [t * inv for t in e]


def axial_rope_tables(rows, rot_dim):
    row = jnp.repeat(jnp.arange(rows, dtype=F32), GRID_W)
    col = jnp.tile(jnp.arange(GRID_W, dtype=F32), rows)
    axis_dim = rot_dim // 2
    inv_freq = ROPE_THETA ** (-jnp.arange(0, axis_dim, 2, dtype=F32) / axis_dim)
    ang = jnp.concatenate([row[:, None] * inv_freq, col[:, None] * inv_freq], axis=-1)
    return jnp.cos(ang), jnp.sin(ang)


def apply_axial_rope(x, tab):
    cos, sin = tab
    quarter = x.shape[-1] // 4
    xr, xc = jnp.split(x.astype(F32), 2, axis=-1)

    def rot(u, cs, sn):
        u1, u2 = jnp.split(u, 2, axis=-1)
        return jnp.concatenate([u1 * cs - u2 * sn, u2 * cs + u1 * sn], axis=-1)

    out = jnp.concatenate([
        rot(xr, cos[:, None, :quarter], sin[:, None, :quarter]),
        rot(xc, cos[:, None, quarter:], sin[:, None, quarter:])], axis=-1)
    return out.astype(x.dtype)


def to_blocks(t):
    b, n = t.shape[:2]
    return jnp.moveaxis(t.reshape(b, n // BLOCK, BLOCK, *t.shape[2:]), 1, 0)


def from_blocks(t):
    t = jnp.moveaxis(t, 0, 1)
    return t.reshape(t.shape[0], -1, *t.shape[3:])


def swiglu(h, w13, w2):
    gate, up = jnp.split(h @ w13, 2, axis=-1)
    return (jax.nn.silu(gate) * up) @ w2


def lambda_init(layer_idx):
    return 0.8 - 0.6 * math.exp(-0.3 * layer_idx)


def window_gqa_latent(q, k, v, k_ctx, v_ctx, sink):
    bsz, n, _, dh = q.shape
    nb = n // BLOCK
    g = A_HEADS // A_KV_HEADS
    scale = dh ** -0.5
    qb = q.reshape(bsz, nb, BLOCK, A_KV_HEADS, g, dh)

    def band(t):
        tp = jnp.pad(t, ((0, 0), (BLOCK, BLOCK), (0, 0), (0, 0)))
        tp = tp.reshape(bsz, nb + 2, BLOCK, A_KV_HEADS, dh)
        return jnp.concatenate([tp[:, :-2], tp[:, 1:-1], tp[:, 2:]], axis=2)

    kb, vb = band(k), band(v)
    s_loc = jnp.einsum('bnqhgd,bnkhd->bhgnqk', qb, kb).astype(F32) * scale
    s_ctx = jnp.einsum('bnqhgd,bchd->bhgnqc', qb, k_ctx).astype(F32) * scale
    qi = jnp.arange(BLOCK)[:, None]
    kj = jnp.arange(3 * BLOCK)[None, :]
    key_pos = (jnp.arange(nb)[:, None, None] - 1) * BLOCK + kj[None]
    valid = (jnp.abs(kj - BLOCK - qi) <= WINDOW)[None] & (key_pos >= 0) & (key_pos < n)
    s_loc = jnp.where(valid, s_loc, NEG_INF)
    s_sink = sink.astype(F32).reshape(1, A_KV_HEADS, g, 1, 1, 1)
    p_loc, p_ctx, _ = joint_softmax(s_loc, s_ctx, s_sink)
    o = (jnp.einsum('bhgnqk,bnkhd->bnqhgd', p_loc.astype(v.dtype), vb)
         + jnp.einsum('bhgnqc,bchd->bnqhgd', p_ctx.astype(v.dtype), v_ctx))
    return o.reshape(bsz, n, A_HEADS * dh)


def gqa_ctx(q, k, v, sink):
    bsz, t, _, dh = q.shape
    g = A_HEADS // A_KV_HEADS
    qg = q.reshape(bsz, t, A_KV_HEADS, g, dh)
    s = jnp.einsum('bqhgd,bkhd->bhgqk', qg, k).astype(F32) * dh ** -0.5
    p, _ = joint_softmax(s, sink.astype(F32).reshape(1, A_KV_HEADS, g, 1, 1))
    o = jnp.einsum('bhgqk,bkhd->bqhgd', p.astype(v.dtype), v)
    return o.reshape(bsz, t, A_HEADS * dh)


def diff_attend(q, kv_parts, lam):
    scale = q.shape[-1] ** -0.5
    logits = [jnp.einsum('bqhmd,bkhmd->bmhqk', q, k).astype(F32) * scale for k, _ in kv_parts]
    probs = joint_softmax(*logits)
    outs = [jnp.einsum('bhqk,bkhd->bqhd', (p[:, 0] - lam * p[:, 1]).astype(v.dtype), v)
            for p, (_, v) in zip(probs, kv_parts)]
    return functools.reduce(jnp.add, outs)


def diff_out(o, subln_g, lam_init):
    b, t = o.shape[:2]
    return (rmsnorm(o, subln_g) * (1.0 - lam_init)).reshape(b, t, -1)


def ab_queries(p_q):
    b, t = p_q.shape[:2]
    qa = p_q[..., :A_Q_W].reshape(b, t, A_HEADS, A_HEAD_DIM)
    qb = p_q[..., A_Q_W:].reshape(b, t, B_HEADS, 2, B_QK_DIM)
    return qa, qb


def ab_keys_values(p_kv):
    b, t = p_kv.shape[:2]
    o1 = A_KV_W
    o2 = 2 * A_KV_W
    o3 = o2 + B_QK_W
    ka = p_kv[..., :o1].reshape(b, t, A_KV_HEADS, A_HEAD_DIM)
    va = p_kv[..., o1:o2].reshape(b, t, A_KV_HEADS, A_HEAD_DIM)
    kb = p_kv[..., o2:o3].reshape(b, t, B_HEADS, 2, B_QK_DIM)
    vb = p_kv[..., o3:].reshape(b, t, B_HEADS, B_V_DIM)
    return ka, va, kb, vb


def rope_diff(t, tab):
    b, n = t.shape[:2]
    return apply_axial_rope(t.reshape(b, n, B_HEADS * 2, B_QK_DIM), tab).reshape(t.shape)


def mixer_ab(hx, hc, w_in, w_out, sink, lam_vec, subln_g, lam_init, tab, ctx_queries):
    p = hx @ w_in
    qa, qb = ab_queries(p[..., :AB_Q_W])
    ka, va, kb, vb = ab_keys_values(p[..., AB_Q_W:])
    qa, ka = apply_axial_rope(qa, tab), apply_axial_rope(ka, tab)
    qb, kb = rope_diff(qb, tab), rope_diff(kb, tab)
    if ctx_queries:
        pc = hc @ w_in
        cqa, cqb = ab_queries(pc[..., :AB_Q_W])
        cka, cva, ckb, cvb = ab_keys_values(pc[..., AB_Q_W:])
    else:
        cka, cva, ckb, cvb = ab_keys_values(hc @ w_in[:, AB_Q_W:])
    lv = lam_vec.astype(F32)
    lam = jnp.exp(jnp.sum(lv[0] * lv[1])) - jnp.exp(jnp.sum(lv[2] * lv[3])) + lam_init

    oa = window_gqa_latent(qa, ka, va, cka, cva, sink)
    ob = from_blocks(lax.map(lambda qblk: diff_attend(qblk, [(kb, vb), (ckb, cvb)], lam),
                             to_blocks(qb)))
    yx = jnp.concatenate([oa, diff_out(ob, subln_g, lam_init)], axis=-1) @ w_out
    if not ctx_queries:
        return yx, None
    oca = gqa_ctx(cqa, cka, cva, sink)
    ocb = diff_out(diff_attend(cqb, [(ckb, cvb)], lam), subln_g, lam_init)
    yc = jnp.concatenate([oca, ocb], axis=-1) @ w_out
    return yx, yc


def mla_queries(p_q, q_norm_g, wq_b):
    b, t = p_q.shape[:2]
    q = (rmsnorm(p_q, q_norm_g) @ wq_b).reshape(b, t, C_HEADS, C_NOPE + C_ROPE)
    return q[..., :C_NOPE], q[..., C_NOPE:]


def mla_keys_values(p_kv, kv_norm_g, wkv_b):
    b, t = p_kv.shape[:2]
    kv_lat, k_rope = p_kv[..., :C_KV_LORA], p_kv[..., C_KV_LORA:]
    kv = (rmsnorm(kv_lat, kv_norm_g) @ wkv_b).reshape(b, t, C_HEADS, C_NOPE + C_V)
    return kv[..., :C_NOPE], k_rope, kv[..., C_NOPE:]


def mla_attend(q_nope, q_rope, kv_parts):
    scale = (C_NOPE + C_ROPE) ** -0.5
    logits = [(jnp.einsum('bqhd,bkhd->bhqk', q_nope, kn)
               + jnp.einsum('bqhr,bkr->bhqk', q_rope, kr)).astype(F32) * scale
              for kn, kr, _ in kv_parts]
    probs = joint_softmax(*logits)
    outs = [jnp.einsum('bhqk,bkhd->bqhd', p.astype(v.dtype), v) for p, (_, _, v) in zip(probs, kv_parts)]
    return functools.reduce(jnp.add, outs)


def mixer_mla(hx, hc, w_in, q_norm_g, kv_norm_g, wq_b, wkv_b, w_out, tab, ctx_queries):
    bsz, n, _ = hx.shape
    p = hx @ w_in
    qn, qr = mla_queries(p[..., :C_Q_LORA], q_norm_g, wq_b)
    kn, kr, v = mla_keys_values(p[..., C_Q_LORA:], kv_norm_g, wkv_b)
    qr = apply_axial_rope(qr, tab)
    kr = apply_axial_rope(kr[:, :, None, :], tab)[:, :, 0, :]
    if ctx_queries:
        pc = hc @ w_in
        cqn, cqr = mla_queries(pc[..., :C_Q_LORA], q_norm_g, wq_b)
        ckn, ckr, cv = mla_keys_values(pc[..., C_Q_LORA:], kv_norm_g, wkv_b)
    else:
        ckn, ckr, cv = mla_keys_values(hc @ w_in[:, C_Q_LORA:], kv_norm_g, wkv_b)
    o = from_blocks(lax.map(lambda qs: mla_attend(qs[0], qs[1], [(kn, kr, v), (ckn, ckr, cv)]),
                            (to_blocks(qn), to_blocks(qr))))
    yx = o.reshape(bsz, n, C_OUT_W) @ w_out
    if not ctx_queries:
        return yx, None
    oc = mla_attend(cqn, cqr, [(ckn, ckr, cv)])
    yc = oc.reshape(bsz, hc.shape[1], C_OUT_W) @ w_out
    return yx, yc


def setup_inputs(seed: int = 0) -> dict:
    key = jax.random.key(seed)
    ks = jax.random.split(key, 20)

    def nrm(k, shape, scale):
        return jax.random.normal(k, shape, F32) * scale

    d = D_MODEL
    return {
        'x': nrm(ks[0], (BATCH, SEQ, d), 1.0),
        'c': nrm(ks[1], (BATCH, d), 1.0),
        'ctx': nrm(ks[2], (BATCH, CTX_LEN, d), 1.0),
        'c_ctx': nrm(ks[3], (d,), 1.0),
        'ada_w': nrm(ks[4], (DEPTH, d, 6 * d), 0.5 * d ** -0.5),
        'ada_b': nrm(ks[5], (DEPTH, 6 * d), 0.01),
        'norm_g': 1.0 + nrm(ks[6], (DEPTH, 4, d), 0.05),
        'ffn_w13': nrm(ks[7], (DEPTH, d, 2 * FFN_HIDDEN), d ** -0.5),
        'ffn_w2': nrm(ks[8], (DEPTH, FFN_HIDDEN, d), FFN_HIDDEN ** -0.5),
        'ab_w_in': nrm(ks[9], (N_EVEN, d, AB_IN_W), d ** -0.5),
        'ab_w_out': nrm(ks[10], (N_EVEN, AB_OUT_W, d), AB_OUT_W ** -0.5),
        'ab_sink': nrm(ks[11], (N_EVEN, A_HEADS), 0.5),
        'diff_lambda': nrm(ks[12], (N_EVEN, 4, B_QK_DIM), 0.1),
        'diff_subln_g': 1.0 + nrm(ks[13], (N_EVEN, B_V_DIM), 0.05),
        'mla_w_in': nrm(ks[14], (N_ODD, d, C_IN_W), d ** -0.5),
        'mla_q_norm_g': 1.0 + nrm(ks[15], (N_ODD, C_Q_LORA), 0.05),
        'mla_kv_norm_g': 1.0 + nrm(ks[16], (N_ODD, C_KV_LORA), 0.05),
        'mla_wq_b': nrm(ks[17], (N_ODD, C_Q_LORA, C_HEADS * (C_NOPE + C_ROPE)), C_Q_LORA ** -0.5),
        'mla_wkv_b': nrm(ks[18], (N_ODD, C_KV_LORA, C_HEADS * (C_NOPE + C_V)), C_KV_LORA ** -0.5),
        'mla_w_out': nrm(ks[19], (N_ODD, C_OUT_W, d), C_OUT_W ** -0.5),
    }


def reference(x, c, ctx, c_ctx, ada_w, ada_b, norm_g, ffn_w13, ffn_w2,
              ab_w_in, ab_w_out, ab_sink, diff_lambda, diff_subln_g,
              mla_w_in, mla_q_norm_g, mla_kv_norm_g, mla_wq_b, mla_wkv_b, mla_w_out):
    bsz, n, d = x.shape
    ROWS = n // GRID_W
    tab_ab = axial_rope_tables(ROWS, A_HEAD_DIM)
    tab_c = axial_rope_tables(ROWS, C_ROPE)
    sc_x = jax.nn.silu(c)
    sc_c = jax.nn.silu(c_ctx)
    for l in range(DEPTH):
        ctx_out = l < DEPTH - 1
        mx = (sc_x @ ada_w[l] + ada_b[l]).reshape(bsz, 6, 1, d)
        mc = (sc_c @ ada_w[l] + ada_b[l]).reshape(6, d)
        hx = modulate(rmsnorm(x, norm_g[l, 0]), mx[:, 0], mx[:, 1])
        hc = modulate(rmsnorm(ctx, norm_g[l, 0]), mc[0], mc[1])
        if l % 2 == 0:
            e = l // 2
            yx, yc = mixer_ab(hx, hc, ab_w_in[e], ab_w_out[e], ab_sink[e], diff_lambda[e],
                              diff_subln_g[e], lambda_init(l), tab_ab, ctx_out)
        else:
            o = l // 2
            yx, yc = mixer_mla(hx, hc, mla_w_in[o], mla_q_norm_g[o], mla_kv_norm_g[o],
                               mla_wq_b[o], mla_wkv_b[o], mla_w_out[o], tab_c, ctx_out)
        x = x + mx[:, 2] * rmsnorm(yx, norm_g[l, 1])
        hx = modulate(rmsnorm(x, norm_g[l, 2]), mx[:, 3], mx[:, 4])
        x = x + mx[:, 5] * rmsnorm(swiglu(hx, ffn_w13[l], ffn_w2[l]), norm_g[l, 3])
        if ctx_out:
            ctx = ctx + mc[2] * rmsnorm(yc, norm_g[l, 1])
            hc = modulate(rmsnorm(ctx, norm_g[l, 2]), mc[3], mc[4])
            ctx = ctx + mc[5] * rmsnorm(swiglu(hc, ffn_w13[l], ffn_w2[l]), norm_g[l, 3])
    return x
```

```python
import functools
import math

import jax
import jax.numpy as jnp
from jax import lax
from jax.experimental import pallas as pl
from jax.experimental.pallas import tpu as pltpu

F32 = jnp.float32
BF16 = jnp.bfloat16

D_MODEL = 1024
GRID_W = 64
ROPE_THETA = 10000.0
NORM_EPS = 1e-6
NEG_INF = -1e30
WINDOW = 128
LOG2E = 1.4426950408889634

A_HEADS, A_KV_HEADS, A_HEAD_DIM = 8, 2, 64
B_HEADS, B_QK_DIM, B_V_DIM = 4, 64, 128
A_Q_W = A_HEADS * A_HEAD_DIM
A_KV_W = A_KV_HEADS * A_HEAD_DIM
B_QK_W = B_HEADS * 2 * B_QK_DIM
B_V_W = B_HEADS * B_V_DIM
AB_Q_W = A_Q_W + B_QK_W
AB_K_W = A_KV_W + B_QK_W
AB_V_W = A_KV_W + B_V_W

C_HEADS, C_Q_LORA, C_KV_LORA, C_NOPE, C_ROPE, C_V = 16, 384, 256, 64, 32, 64
C_SLAB = 128
C_LAT_W = C_Q_LORA + C_KV_LORA

FFN_HIDDEN = 2816
FFN_CHUNK = 256

KEY_TILE = 256
Q_TILE = 512
ROW_TILE = 512
VMEM_LIMIT = 56 * 1024 * 1024

_NT = (((1,), (1,)), ((), ()))


def _dot(a, b):
    return jnp.dot(a, b, preferred_element_type=F32)


def _dot_nt(a, b):
    return lax.dot_general(a, b, _NT, preferred_element_type=F32)


def _rms(x, g):
    return x * lax.rsqrt(jnp.mean(x * x, axis=-1, keepdims=True) + NORM_EPS) * g


def _const_spec(shape):
    zeros = (0,) * len(shape)
    return pl.BlockSpec(shape, lambda *_: zeros, pipeline_mode=pl.Buffered(1))


def _params(n_axes):
    return pltpu.CompilerParams(dimension_semantics=("arbitrary",) * n_axes,
                                vmem_limit_bytes=VMEM_LIMIT)


def _ada_kernel(c_ref, w_ref, b_ref, o_ref):
    c = c_ref[...]
    s = c / (1.0 + jnp.exp(-c))
    o_ref[0] = _dot(s.astype(BF16), w_ref[0].astype(BF16)) + b_ref[0]


def _ada_modulation(cond, ada_w, ada_b):
    depth, d, n = ada_w.shape
    tn = 1536
    return pl.pallas_call(
        _ada_kernel,
        grid=(depth, n // tn),
        in_specs=[pl.BlockSpec((16, d), lambda l, j: (0, 0)),
                  pl.BlockSpec((1, d, tn), lambda l, j: (l, 0, j)),
                  pl.BlockSpec((1, 1, tn), lambda l, j: (l, 0, j))],
        out_specs=pl.BlockSpec((1, 16, tn), lambda l, j: (l, 0, j)),
        out_shape=jax.ShapeDtypeStruct((depth, 16, n), F32),
        compiler_params=_params(2),
        name="ada_modulation",
    )(cond, ada_w, ada_b.reshape(depth, 1, n))


def _rope_rows(blk, quarter, cf, sf):
    a, b = blk[0:quarter], blk[quarter:2 * quarter]
    c, d = blk[2 * quarter:3 * quarter], blk[3 * quarter:4 * quarter]
    cr, sr = cf[0:quarter], sf[0:quarter]
    cc, sc = cf[quarter:2 * quarter], sf[quarter:2 * quarter]
    return jnp.concatenate([a * cr - b * sr, b * cr + a * sr, c * cc - d * sc, d * cc + c * sc], axis=0)


def _rope_lanes(x, quarter, ct, s1, s2):
    return x * ct + pltpu.roll(x, 128 - quarter, 1) * s1 + pltpu.roll(x, quarter, 1) * s2


def _ab_proj_kernel(x_ref, mod_ref, g_ref, wqt_ref, wk_ref, wvt_ref, cf_ref, sf_ref, ct_ref, s1_ref, s2_ref,
                    qt_ref, k_ref, vt_ref):
    h = _rms(x_ref[0], g_ref[...]) * (1.0 + mod_ref[0, 1:2, :]) + mod_ref[0, 0:1, :]
    hb = h.astype(BF16)
    cf, sf = cf_ref[...], sf_ref[...]
    qt = _dot_nt(wqt_ref[...], hb)
    for g in range(AB_Q_W // 64):
        blk = _rope_rows(qt[64 * g:64 * g + 64], 16, cf, sf)
        qt_ref[0, 64 * g:64 * g + 64, :] = (blk * (LOG2E * 64 ** -0.5)).astype(BF16)
    kk = _dot(hb, wk_ref[...])
    ct, s1, s2 = ct_ref[...], s1_ref[...], s2_ref[...]
    for c in range(AB_K_W // 128):
        k_ref[0, :, 128 * c:128 * c + 128] = _rope_lanes(kk[:, 128 * c:128 * c + 128], 16, ct, s1, s2).astype(BF16)
    vt = _dot_nt(wvt_ref[...], hb)
    for j in range(vt.shape[1] // KEY_TILE):
        vt_ref[0, j] = vt[:, KEY_TILE * j:KEY_TILE * j + KEY_TILE].astype(BF16)


def _ab_project(x, mod, g, wqt, wk, wvt, tabs, tm):
    bsz, rows, d = x.shape
    cf, sf, ct, s1, s2 = tabs
    mod_map = (lambda b, i: (b, 0, 0)) if mod.shape[0] == bsz else (lambda b, i: (0, 0, 0))
    return pl.pallas_call(
        _ab_proj_kernel,
        grid=(bsz, rows // tm),
        in_specs=[pl.BlockSpec((1, tm, d), lambda b, i: (b, i, 0)),
                  pl.BlockSpec((1, 6, d), mod_map),
                  _const_spec((1, d)), _const_spec(wqt.shape), _const_spec(wk.shape), _const_spec(wvt.shape),
                  pl.BlockSpec((32, tm), lambda b, i: (0, i)),
                  pl.BlockSpec((32, tm), lambda b, i: (0, i)),
                  pl.BlockSpec((tm, 128), lambda b, i: (i, 0)),
                  pl.BlockSpec((tm, 128), lambda b, i: (i, 0)),
                  pl.BlockSpec((tm, 128), lambda b, i: (i, 0))],
        out_specs=[pl.BlockSpec((1, AB_Q_W, tm), lambda b, i: (b, 0, i)),
                   pl.BlockSpec((1, tm, AB_K_W), lambda b, i: (b, i, 0)),
                   pl.BlockSpec((1, tm // KEY_TILE, AB_V_W, KEY_TILE), lambda b, i: (b, i, 0, 0))],
        out_shape=[jax.ShapeDtypeStruct((bsz, AB_Q_W, rows), BF16),
                   jax.ShapeDtypeStruct((bsz, rows, AB_K_W), BF16),
                   jax.ShapeDtypeStruct((bsz, rows // KEY_TILE, AB_V_W, KEY_TILE), BF16)],
        compiler_params=_params(2),
        name="ab_project",
    )(x, mod, g, wqt, wk, wvt, cf, sf, ct, s1, s2)


def _softmax_step(k_tile, v_tile, q_pad, m, l, acc_ref, valid=None):
    s = _dot(k_tile, q_pad)
    if valid is not None:
        s = jnp.where(valid, s, NEG_INF)
    m_new = jnp.maximum(m, jnp.max(s, axis=0, keepdims=True))
    p = jnp.exp2(s - m_new)
    alpha = jnp.exp2(m - m_new)
    l_new = alpha * l + jnp.sum(p, axis=0, keepdims=True)
    acc_ref[...] = alpha * acc_ref[...] + _dot(v_tile, p.astype(BF16))
    return m_new, l_new


def _attend(q_pad, k_parts, v_parts, acc_ref, lanes=slice(None), rows=slice(None)):
    tq = q_pad.shape[1]
    acc_ref[...] = jnp.zeros(acc_ref.shape, F32)
    m = jnp.full((1, tq), NEG_INF, F32)
    l = jnp.zeros((1, tq), F32)
    for k_ref, v_ref in zip(k_parts, v_parts):
        n_tiles = v_ref.shape[1]

        def body(t, carry, k_ref=k_ref, v_ref=v_ref):
            start = pl.multiple_of(t * KEY_TILE, KEY_TILE)
            return _softmax_step(k_ref[0, pl.ds(start, KEY_TILE), lanes], v_ref[0, t, rows, :], q_pad,
                                 carry[0], carry[1], acc_ref)

        if n_tiles == 1:
            m, l = _softmax_step(k_ref[0, :, lanes], v_ref[0, 0, rows, :], q_pad, m, l, acc_ref)
        else:
            m, l = lax.fori_loop(0, n_tiles, body, (m, l))
    return acc_ref[...] * (1.0 / l)


def _split_parts(refs, n_parts):
    return refs[:n_parts], refs[n_parts:2 * n_parts], refs[2 * n_parts:]


def _diff_attn_kernel(lam_init, n_parts, qt_ref, *refs):
    k_parts, v_parts, (lam_ref, g_ref, o_ref, acc_ref) = _split_parts(refs, n_parts)
    q = qt_ref[0]
    row = lax.broadcasted_iota(jnp.int32, q.shape, 0)
    zero = jnp.zeros_like(q)
    o0 = _attend(jnp.where(row < B_QK_DIM, q, zero), k_parts, v_parts, acc_ref)
    o1 = _attend(jnp.where(row >= B_QK_DIM, q, zero), k_parts, v_parts, acc_ref)
    lv = lam_ref[...]
    lam = (jnp.exp(jnp.sum(lv[0:1] * lv[1:2], axis=1, keepdims=True))
           - jnp.exp(jnp.sum(lv[2:3] * lv[3:4], axis=1, keepdims=True)) + lam_init)
    dlt = o0 - lam * o1
    y = dlt * lax.rsqrt(jnp.mean(dlt * dlt, axis=0, keepdims=True) + NORM_EPS) * g_ref[...] * (1.0 - lam_init)
    o_ref[0] = y.T.astype(BF16)


def _diff_attention(qt, k_parts, v_parts, lam_vec, subln_g, lam_init, tq):
    bsz, _, rows = qt.shape
    in_specs = [pl.BlockSpec((1, 128, tq), lambda b, h, i: (b, A_Q_W // 128 + h, i))]
    for k in k_parts:
        in_specs.append(pl.BlockSpec((1, k.shape[1], 128), lambda b, h, i: (b, 0, A_KV_W // 128 + h)))
    for v in v_parts:
        in_specs.append(pl.BlockSpec((1, v.shape[1], 128, KEY_TILE), lambda b, h, i: (b, 0, A_KV_W // 128 + h, 0)))
    in_specs += [pl.BlockSpec(lam_vec.shape, lambda b, h, i: (0, 0)),
                 pl.BlockSpec((B_V_DIM, 1), lambda b, h, i: (0, 0))]
    return pl.pallas_call(
        functools.partial(_diff_attn_kernel, lam_init, len(k_parts)),
        grid=(bsz, B_HEADS, rows // tq),
        in_specs=in_specs,
        out_specs=pl.BlockSpec((1, tq, 128), lambda b, h, i: (b, i, h)),
        out_shape=jax.ShapeDtypeStruct((bsz, rows, B_V_W), BF16),
        scratch_shapes=[pltpu.VMEM((B_V_DIM, tq), F32)],
        compiler_params=_params(3),
        name="diff_attention",
    )(qt, *k_parts, *v_parts, lam_vec, subln_g.reshape(B_V_DIM, 1))


def _window_attn_kernel(windowed, n_parts, sink_ref, qt_ref, *refs):
    k_parts, v_parts, (o_ref, acc_ref, pair_ref) = _split_parts(refs, n_parts)
    tq = qt_ref.shape[2]
    i = pl.program_id(1)
    group = A_HEADS // A_KV_HEADS
    zeros = jnp.zeros((A_HEAD_DIM, tq), BF16)
    if windowed:
        n_lat = v_parts[0].shape[1]
        per_q = tq // KEY_TILE
        lo = jnp.maximum(i * per_q - 1, 0)
        hi = jnp.minimum((i + 1) * per_q + 1, n_lat)
        rel = (lax.broadcasted_iota(jnp.int32, (KEY_TILE, tq), 1)
               - lax.broadcasted_iota(jnp.int32, (KEY_TILE, tq), 0))
    for hq in range(A_HEADS):
        kv = hq // group
        qh = qt_ref[0, A_HEAD_DIM * hq:A_HEAD_DIM * (hq + 1), :]
        q_pad = jnp.concatenate([qh, zeros] if kv == 0 else [zeros, qh], axis=0)
        rows = slice(A_HEAD_DIM * kv, A_HEAD_DIM * (kv + 1))
        acc_ref[...] = jnp.zeros(acc_ref.shape, F32)
        m = jnp.full((1, tq), sink_ref[hq] * LOG2E, F32)
        l = jnp.ones((1, tq), F32)
        if windowed:
            k_ref, v_ref = k_parts[0], v_parts[0]

            def body(t, carry, k_ref=k_ref, v_ref=v_ref, q_pad=q_pad, rows=rows):
                start = pl.multiple_of(t * KEY_TILE, KEY_TILE)
                valid = jnp.abs(rel + (i * tq - t * KEY_TILE)) <= WINDOW
                return _softmax_step(k_ref[0, pl.ds(start, KEY_TILE), :], v_ref[0, t, rows, :], q_pad,
                                     carry[0], carry[1], acc_ref, valid)

            m, l = lax.fori_loop(lo, hi, body, (m, l))
        k_ref, v_ref = k_parts[-1], v_parts[-1]
        m, l = _softmax_step(k_ref[0], v_ref[0, 0, rows, :], q_pad, m, l, acc_ref)
        half = slice(A_HEAD_DIM * (hq % 2), A_HEAD_DIM * (hq % 2 + 1))
        pair_ref[half, :] = acc_ref[...] * (1.0 / l)
        if hq % 2 == 1:
            o_ref[0, :, 128 * (hq // 2):128 * (hq // 2 + 1)] = pair_ref[...].T.astype(BF16)


def _window_attention(qt, k_parts, v_parts, sink, windowed, tq):
    bsz, _, rows = qt.shape
    in_specs = [pl.BlockSpec(memory_space=pltpu.SMEM),
                pl.BlockSpec((1, A_Q_W, tq), lambda b, i: (b, 0, i))]
    for k in k_parts:
        in_specs.append(pl.BlockSpec((1, k.shape[1], 128), lambda b, i: (b, 0, 0)))
    for v in v_parts:
        in_specs.append(pl.BlockSpec((1, v.shape[1], 128, KEY_TILE), lambda b, i: (b, 0, 0, 0)))
    return pl.pallas_call(
        functools.partial(_window_attn_kernel, windowed, len(k_parts)),
        grid=(bsz, rows // tq),
        in_specs=in_specs,
        out_specs=pl.BlockSpec((1, tq, A_Q_W), lambda b, i: (b, i, 0)),
        out_shape=jax.ShapeDtypeStruct((bsz, rows, A_Q_W), BF16),
        scratch_shapes=[pltpu.VMEM((A_HEAD_DIM, tq), F32), pltpu.VMEM((128, tq), F32)],
        compiler_params=_params(2),
        name="window_attention",
    )(sink, qt, *k_parts, *v_parts)


def _mla_proj_kernel(with_q, x_ref, mod_ref, g_ref, win_ref, wkr_ref, qg_ref, kvg_ref, wqt_ref, wkn_ref, wvt_ref,
                     cf_ref, sf_ref, ct_ref, s1_ref, s2_ref, *out_refs):
    h = _rms(x_ref[0], g_ref[...]) * (1.0 + mod_ref[0, 1:2, :]) + mod_ref[0, 0:1, :]
    hb = h.astype(BF16)
    lat = _dot(hb, win_ref[...])
    kvn = _rms(lat[:, C_Q_LORA:], kvg_ref[...]).astype(BF16)
    if with_q:
        qt_ref, k_ref, vt_ref = out_refs
        qn = _rms(lat[:, :C_Q_LORA], qg_ref[...]).astype(BF16)
        qt = _dot_nt(wqt_ref[...], qn)
        cf, sf = cf_ref[...], sf_ref[...]
        scale = LOG2E * (C_NOPE + C_ROPE) ** -0.5
        for hd in range(C_HEADS):
            base = C_SLAB * hd
            rope = _rope_rows(qt[base + C_NOPE:base + C_NOPE + C_ROPE], C_ROPE // 4, cf, sf)
            blk = jnp.concatenate([qt[base:base + C_NOPE], rope, qt[base + C_NOPE + C_ROPE:base + C_SLAB]], axis=0)
            qt_ref[0, base:base + C_SLAB, :] = (blk * scale).astype(BF16)
    else:
        k_ref, vt_ref = out_refs
    kn = _dot(kvn, wkn_ref[...])
    kr = _rope_lanes(_dot(hb, wkr_ref[...]), C_ROPE // 4, ct_ref[...], s1_ref[...], s2_ref[...])
    for hd in range(C_HEADS):
        k_ref[0, :, C_SLAB * hd:C_SLAB * (hd + 1)] = (kn[:, C_SLAB * hd:C_SLAB * (hd + 1)] + kr).astype(BF16)
    vt = _dot_nt(wvt_ref[...], kvn)
    for j in range(vt.shape[1] // KEY_TILE):
        vt_ref[0, j] = vt[:, KEY_TILE * j:KEY_TILE * j + KEY_TILE].astype(BF16)


def _mla_project(x, mod, g, weights, tabs, with_q, tm):
    bsz, rows, d = x.shape
    win, wkr, qg, kvg, wqt, wkn, wvt = weights
    cf, sf, ct, s1, s2 = tabs
    mod_map = (lambda b, i: (b, 0, 0)) if mod.shape[0] == bsz else (lambda b, i: (0, 0, 0))
    kw, vw = C_HEADS * C_SLAB, C_HEADS * C_V
    out_specs = [pl.BlockSpec((1, tm, kw), lambda b, i: (b, i, 0)),
                 pl.BlockSpec((1, tm // KEY_TILE, vw, KEY_TILE), lambda b, i: (b, i, 0, 0))]
    out_shape = [jax.ShapeDtypeStruct((bsz, rows, kw), BF16),
                 jax.ShapeDtypeStruct((bsz, rows // KEY_TILE, vw, KEY_TILE), BF16)]
    if with_q:
        out_specs.insert(0, pl.BlockSpec((1, kw, tm), lambda b, i: (b, 0, i)))
        out_shape.insert(0, jax.ShapeDtypeStruct((bsz, kw, rows), BF16))
    return pl.pallas_call(
        functools.partial(_mla_proj_kernel, with_q),
        grid=(bsz, rows // tm),
        in_specs=[pl.BlockSpec((1, tm, d), lambda b, i: (b, i, 0)),
                  pl.BlockSpec((1, 6, d), mod_map),
                  _const_spec((1, d)), _const_spec(win.shape), _const_spec(wkr.shape),
                  _const_spec(qg.shape), _const_spec(kvg.shape),
                  _const_spec(wqt.shape), _const_spec(wkn.shape), _const_spec(wvt.shape),
                  pl.BlockSpec((16, tm), lambda b, i: (0, i)),
                  pl.BlockSpec((16, tm), lambda b, i: (0, i)),
                  pl.BlockSpec((tm, 128), lambda b, i: (i, 0)),
                  pl.BlockSpec((tm, 128), lambda b, i: (i, 0)),
                  pl.BlockSpec((tm, 128), lambda b, i: (i, 0))],
        out_specs=out_specs,
        out_shape=out_shape,
        compiler_params=_params(2),
        name="mla_project",
    )(x, mod, g, win, wkr, qg, kvg, wqt, wkn, wvt, cf, sf, ct, s1, s2)


def _mla_attn_kernel(n_parts, qt_ref, *refs):
    k_parts, v_parts, (o_ref, acc_ref, pair_ref) = _split_parts(refs, n_parts)
    for j in range(2):
        o = _attend(qt_ref[0, C_SLAB * j:C_SLAB * (j + 1), :], k_parts, v_parts, acc_ref,
                    lanes=slice(C_SLAB * j, C_SLAB * (j + 1)), rows=slice(C_V * j, C_V * (j + 1)))
        pair_ref[C_V * j:C_V * (j + 1), :] = o
    o_ref[0] = pair_ref[...].T.astype(BF16)


def _mla_attention(qt, k_parts, v_parts, tq):
    bsz, _, rows = qt.shape
    in_specs = [pl.BlockSpec((1, 2 * C_SLAB, tq), lambda b, p, i: (b, p, i))]
    for k in k_parts:
        in_specs.append(pl.BlockSpec((1, k.shape[1], 2 * C_SLAB), lambda b, p, i: (b, 0, p)))
    for v in v_parts:
        in_specs.append(pl.BlockSpec((1, v.shape[1], 2 * C_V, KEY_TILE), lambda b, p, i: (b, 0, p, 0)))
    return pl.pallas_call(
        functools.partial(_mla_attn_kernel, len(k_parts)),
        grid=(bsz, C_HEADS // 2, rows // tq),
        in_specs=in_specs,
        out_specs=pl.BlockSpec((1, tq, 2 * C_V), lambda b, p, i: (b, i, p)),
        out_shape=jax.ShapeDtypeStruct((bsz, rows, C_HEADS * C_V), BF16),
        scratch_shapes=[pltpu.VMEM((C_V, tq), F32), pltpu.VMEM((2 * C_V, tq), F32)],
        compiler_params=_params(3),
        name="mla_attention",
    )(qt, *k_parts, *v_parts)


def _mix_ffn_kernel(n_o, *refs):
    o_refs = refs[:n_o]
    x_ref, mod_ref, g_ref, wout_ref, w13_ref, w2_ref, out_ref = refs[n_o:]
    y = None
    row = 0
    for o_ref in o_refs:
        width = o_ref.shape[2]
        part = _dot(o_ref[0], wout_ref[row:row + width, :])
        y = part if y is None else y + part
        row += width
    x1 = x_ref[0] + mod_ref[0, 2:3, :] * _rms(y, g_ref[1:2, :])
    h = _rms(x1, g_ref[2:3, :]) * (1.0 + mod_ref[0, 4:5, :]) + mod_ref[0, 3:4, :]
    hb = h.astype(BF16)
    f = None
    for c in range(FFN_HIDDEN // FFN_CHUNK):
        lo = FFN_CHUNK * c
        gate = _dot(hb, w13_ref[:, lo:lo + FFN_CHUNK])
        up = _dot(hb, w13_ref[:, FFN_HIDDEN + lo:FFN_HIDDEN + lo + FFN_CHUNK])
        act = (gate / (1.0 + jnp.exp(-gate)) * up).astype(BF16)
        part = _dot(act, w2_ref[lo:lo + FFN_CHUNK, :])
        f = part if f is None else f + part
    out_ref[0] = x1 + mod_ref[0, 5:6, :] * _rms(f, g_ref[3:4, :])


def _mix_ffn(o_parts, x, mod, g4, wout, w13, w2, tm):
    bsz, rows, d = x.shape
    mod_map = (lambda b, i: (b, 0, 0)) if mod.shape[0] == bsz else (lambda b, i: (0, 0, 0))
    in_specs = [pl.BlockSpec((1, tm, o.shape[2]), lambda b, i: (b, i, 0)) for o in o_parts]
    in_specs += [pl.BlockSpec((1, tm, d), lambda b, i: (b, i, 0)),
                 pl.BlockSpec((1, 6, d), mod_map),
                 _const_spec(g4.shape), _const_spec(wout.shape), _const_spec(w13.shape), _const_spec(w2.shape)]
    return pl.pallas_call(
        functools.partial(_mix_ffn_kernel, len(o_parts)),
        grid=(bsz, rows // tm),
        in_specs=in_specs,
        out_specs=pl.BlockSpec((1, tm, d), lambda b, i: (b, i, 0)),
        out_shape=jax.ShapeDtypeStruct((bsz, rows, d), F32),
        compiler_params=_params(2),
        name="mix_ffn",
    )(*o_parts, x, mod, g4, wout, w13, w2)


def _rope_tables(n, rot_dim):
    pos = jnp.arange(n, dtype=jnp.int32)
    row = (pos // GRID_W).astype(F32)
    col = (pos % GRID_W).astype(F32)
    axis_dim = rot_dim // 2
    inv_freq = ROPE_THETA ** (-jnp.arange(0, axis_dim, 2, dtype=F32) / axis_dim)
    ang = jnp.concatenate([row[:, None] * inv_freq, col[:, None] * inv_freq], axis=-1)
    return jnp.cos(ang), jnp.sin(ang)


def _lane_tables(cos, sin, lane_dims, quarter):
    dims = jnp.asarray(lane_dims, dtype=jnp.int32)
    live = dims >= 0
    safe = jnp.where(live, dims, 0)
    axis, within = safe // (2 * quarter), safe % (2 * quarter)
    idx = axis * quarter + within % quarter
    first = within < quarter
    c = jnp.where(live[None, :], cos[:, idx], 1.0)
    s = jnp.where(live[None, :], sin[:, idx], 0.0)
    return c, jnp.where(first[None, :], -s, 0.0), jnp.where(first[None, :], 0.0, s)


def _identity_tables(n, half):
    return (jnp.ones((half, n), F32), jnp.zeros((half, n), F32),
            jnp.ones((n, 128), F32), jnp.zeros((n, 128), F32), jnp.zeros((n, 128), F32))


def kernel(x, c, ctx, c_ctx, ada_w, ada_b, norm_g, ffn_w13, ffn_w2, ab_w_in, ab_w_out, ab_sink, diff_lambda,
           diff_subln_g, mla_w_in, mla_q_norm_g, mla_kv_norm_g, mla_wq_b, mla_wkv_b, mla_w_out):
    bsz, n, d = x.shape
    n_ctx = ctx.shape[1]
    depth = ada_w.shape[0]
    assert n_ctx == KEY_TILE and n % Q_TILE == 0 and n % ROW_TILE == 0 and bsz + 1 <= 16

    cond = jnp.zeros((16, d), F32).at[:bsz].set(c).at[bsz].set(c_ctx)
    mods = _ada_modulation(cond, ada_w, ada_b).reshape(depth, 16, 6, d)

    cos_ab, sin_ab = _rope_tables(n, A_HEAD_DIM)
    ab_lane = [l % 64 for l in range(128)]
    tabs_ab = (cos_ab.T, sin_ab.T) + _lane_tables(cos_ab, sin_ab, ab_lane, 16)
    cos_c, sin_c = _rope_tables(n, C_ROPE)
    c_lane = [l - C_NOPE if C_NOPE <= l < C_NOPE + C_ROPE else -1 for l in range(128)]
    tabs_c = (cos_c.T, sin_c.T) + _lane_tables(cos_c, sin_c, c_lane, C_ROPE // 4)

    for l in range(depth):
        last = l == depth - 1
        mod_x, mod_c = mods[l, :bsz], mods[l, bsz:bsz + 1]
        g_in = norm_g[l, 0:1]
        w13, w2 = ffn_w13[l].astype(BF16), ffn_w2[l].astype(BF16)
        if l % 2 == 0:
            e = l // 2
            w = ab_w_in[e]
            o1, o2 = AB_Q_W + A_KV_W, AB_Q_W + 2 * A_KV_W
            o3 = o2 + B_QK_W
            wqt = w[:, :AB_Q_W].T.astype(BF16)
            wk = jnp.concatenate([w[:, AB_Q_W:o1], w[:, o2:o3]], axis=1).astype(BF16)
            wvt = jnp.concatenate([w[:, o1:o2], w[:, o3:]], axis=1).T.astype(BF16)
            wout = ab_w_out[e].astype(BF16)
            lam_init = 0.8 - 0.6 * math.exp(-0.3 * l)
            qt_x, k_x, vt_x = _ab_project(x, mod_x, g_in, wqt, wk, wvt, tabs_ab, ROW_TILE)
            qt_c, k_c, vt_c = _ab_project(ctx, mod_c, g_in, wqt, wk, wvt, _identity_tables(n_ctx, 32), n_ctx)
            oa = _window_attention(qt_x, [k_x, k_c], [vt_x, vt_c], ab_sink[e], True, Q_TILE)
            ob = _diff_attention(qt_x, [k_x, k_c], [vt_x, vt_c], diff_lambda[e], diff_subln_g[e], lam_init, Q_TILE)
            o_x = [oa, ob]
            if not last:
                oca = _window_attention(qt_c, [k_c], [vt_c], ab_sink[e], False, n_ctx)
                ocb = _diff_attention(qt_c, [k_c], [vt_c], diff_lambda[e], diff_subln_g[e], lam_init, n_ctx)
                o_c = [oca, ocb]
        else:
            o = l // 2
            w = mla_w_in[o]
            win = w[:, :C_LAT_W].astype(BF16)
            wkr = jnp.zeros((d, C_SLAB), F32).at[:, C_NOPE:C_NOPE + C_ROPE].set(w[:, C_LAT_W:]).astype(BF16)
            wq = mla_wq_b[o].reshape(C_Q_LORA, C_HEADS, C_NOPE + C_ROPE)
            wq = jnp.pad(wq, ((0, 0), (0, 0), (0, C_SLAB - C_NOPE - C_ROPE)))
            wqt = wq.reshape(C_Q_LORA, C_HEADS * C_SLAB).T.astype(BF16)
            wkv = mla_wkv_b[o].reshape(C_KV_LORA, C_HEADS, C_NOPE + C_V)
            wkn = jnp.pad(wkv[:, :, :C_NOPE], ((0, 0), (0, 0), (0, C_SLAB - C_NOPE)))
            wkn = wkn.reshape(C_KV_LORA, C_HEADS * C_SLAB).astype(BF16)
            wvt = wkv[:, :, C_NOPE:].reshape(C_KV_LORA, C_HEADS * C_V).T.astype(BF16)
            weights = (win, wkr, mla_q_norm_g[o:o + 1], mla_kv_norm_g[o:o + 1], wqt, wkn, wvt)
            wout = mla_w_out[o].astype(BF16)
            qt_x, k_x, vt_x = _mla_project(x, mod_x, g_in, weights, tabs_c, True, ROW_TILE)
            ident = _identity_tables(n_ctx, 16)
            if last:
                k_c, vt_c = _mla_project(ctx, mod_c, g_in, weights, ident, False, n_ctx)
            else:
                qt_c, k_c, vt_c = _mla_project(ctx, mod_c, g_in, weights, ident, True, n_ctx)
            o_x = [_mla_attention(qt_x, [k_x, k_c], [vt_x, vt_c], Q_TILE)]
            if not last:
                o_c = [_mla_attention(qt_c, [k_c], [vt_c], n_ctx)]
        x = _mix_ffn(o_x, x, mod_x, norm_g[l], wout, w13, w2, ROW_TILE)
        if not last:
            ctx = _mix_ffn(o_c, ctx, mod_c, norm_g[l], wout, w13, w2, n_ctx)
    return x
```

```python
import functools
import math

import jax
import jax.numpy as jnp
from jax import lax
from jax.experimental import pallas as pl
from jax.experimental.pallas import tpu as pltpu

F32 = jnp.float32
BF16 = jnp.bfloat16

D_MODEL = 1024
GRID_W = 64
ROPE_THETA = 10000.0
NORM_EPS = 1e-6
NEG_INF = -1e30
WINDOW = 128
LOG2E = 1.4426950408889634

A_HEADS, A_KV_HEADS, A_HEAD_DIM = 8, 2, 64
B_HEADS, B_QK_DIM, B_V_DIM = 4, 64, 128
A_Q_W = A_HEADS * A_HEAD_DIM
A_KV_W = A_KV_HEADS * A_HEAD_DIM
B_QK_W = B_HEADS * 2 * B_QK_DIM
B_V_W = B_HEADS * B_V_DIM
AB_Q_W = A_Q_W + B_QK_W
AB_K_W = A_KV_W + B_QK_W
AB_V_W = A_KV_W + B_V_W

C_HEADS, C_Q_LORA, C_KV_LORA, C_NOPE, C_ROPE, C_V = 16, 384, 256, 64, 32, 64
C_SLAB = 128
C_LAT_W = C_Q_LORA + C_KV_LORA

FFN_HIDDEN = 2816
FFN_CHUNK = 256

KEY_TILE = 256
Q_TILE = 512
ROW_TILE = 512
VMEM_LIMIT = 56 * 1024 * 1024

_NT = (((1,), (1,)), ((), ()))


def _dot(a, b):
    return jnp.dot(a, b, preferred_element_type=F32)


def _dot_nt(a, b):
    return lax.dot_general(a, b, _NT, preferred_element_type=F32)


def _rms(x, g):
    return x * lax.rsqrt(jnp.mean(x * x, axis=-1, keepdims=True) + NORM_EPS) * g


def _const_spec(shape):
    zeros = (0,) * len(shape)
    return pl.BlockSpec(shape, lambda *_: zeros, pipeline_mode=pl.Buffered(1))


def _params(n_axes):
    return pltpu.CompilerParams(dimension_semantics=("arbitrary",) * n_axes,
                                vmem_limit_bytes=VMEM_LIMIT)


def _ada_kernel(c_ref, w_ref, b_ref, o_ref):
    c = c_ref[...]
    s = c / (1.0 + jnp.exp(-c))
    o_ref[0] = _dot(s.astype(BF16), w_ref[0].astype(BF16)) + b_ref[0]


def _ada_modulation(cond, ada_w, ada_b):
    depth, d, n = ada_w.shape
    tn = 1536
    return pl.pallas_call(
        _ada_kernel,
        grid=(depth, n // tn),
        in_specs=[pl.BlockSpec((16, d), lambda l, j: (0, 0)),
                  pl.BlockSpec((1, d, tn), lambda l, j: (l, 0, j)),
                  pl.BlockSpec((1, 1, tn), lambda l, j: (l, 0, j))],
        out_specs=pl.BlockSpec((1, 16, tn), lambda l, j: (l, 0, j)),
        out_shape=jax.ShapeDtypeStruct((depth, 16, n), F32),
        compiler_params=_params(2),
        name="ada_modulation",
    )(cond, ada_w, ada_b.reshape(depth, 1, n))


def _rope_rows(blk, quarter, cf, sf):
    a, b = blk[0:quarter], blk[quarter:2 * quarter]
    c, d = blk[2 * quarter:3 * quarter], blk[3 * quarter:4 * quarter]
    cr, sr = cf[0:quarter], sf[0:quarter]
    cc, sc = cf[quarter:2 * quarter], sf[quarter:2 * quarter]
    return jnp.concatenate([a * cr - b * sr, b * cr + a * sr, c * cc - d * sc, d * cc + c * sc], axis=0)


def _rope_lanes(x, quarter, ct, s1, s2):
    return x * ct + pltpu.roll(x, 128 - quarter, 1) * s1 + pltpu.roll(x, quarter, 1) * s2


def _ab_proj_kernel(x_ref, mod_ref, g_ref, wqt_ref, wk_ref, wvt_ref, cf_ref, sf_ref, ct_ref, s1_ref, s2_ref,
                    qt_ref, ka_ref, kb_ref, vta_ref, vtb_ref):
    h = _rms(x_ref[0], g_ref[...]) * (1.0 + mod_ref[0, 1:2, :]) + mod_ref[0, 0:1, :]
    hb = h.astype(BF16)
    cf, sf = cf_ref[...], sf_ref[...]
    qt = _dot_nt(wqt_ref[...], hb)
    for g in range(AB_Q_W // 64):
        blk = _rope_rows(qt[64 * g:64 * g + 64], 16, cf, sf)
        qt_ref[0, 64 * g:64 * g + 64, :] = (blk * (LOG2E * 64 ** -0.5)).astype(BF16)
    kk = _dot(hb, wk_ref[...])
    ct, s1, s2 = ct_ref[...], s1_ref[...], s2_ref[...]
    ka_ref[0] = _rope_lanes(kk[:, :A_KV_W], 16, ct, s1, s2).astype(BF16)
    for c in range(B_QK_W // 128):
        lo = A_KV_W + 128 * c
        kb_ref[0, :, 128 * c:128 * c + 128] = _rope_lanes(kk[:, lo:lo + 128], 16, ct, s1, s2).astype(BF16)
    vt = _dot_nt(wvt_ref[...], hb)
    for j in range(vt.shape[1] // KEY_TILE):
        tile = vt[:, KEY_TILE * j:KEY_TILE * j + KEY_TILE].astype(BF16)
        vta_ref[0, j] = tile[:A_KV_W]
        vtb_ref[0, j] = tile[A_KV_W:]


def _ab_project(x, mod, g, wqt, wk, wvt, tabs, tm):
    bsz, rows, d = x.shape
    cf, sf, ct, s1, s2 = tabs
    mod_map = (lambda b, i: (b, 0, 0)) if mod.shape[0] == bsz else (lambda b, i: (0, 0, 0))
    return pl.pallas_call(
        _ab_proj_kernel,
        grid=(bsz, rows // tm),
        in_specs=[pl.BlockSpec((1, tm, d), lambda b, i: (b, i, 0)),
                  pl.BlockSpec((1, 6, d), mod_map),
                  _const_spec((1, d)), _const_spec(wqt.shape), _const_spec(wk.shape), _const_spec(wvt.shape),
                  pl.BlockSpec((32, tm), lambda b, i: (0, i)),
                  pl.BlockSpec((32, tm), lambda b, i: (0, i)),
                  pl.BlockSpec((tm, 128), lambda b, i: (i, 0)),
                  pl.BlockSpec((tm, 128), lambda b, i: (i, 0)),
                  pl.BlockSpec((tm, 128), lambda b, i: (i, 0))],
        out_specs=[pl.BlockSpec((1, AB_Q_W, tm), lambda b, i: (b, 0, i)),
                   pl.BlockSpec((1, tm, A_KV_W), lambda b, i: (b, i, 0)),
                   pl.BlockSpec((1, tm, B_QK_W), lambda b, i: (b, i, 0)),
                   pl.BlockSpec((1, tm // KEY_TILE, A_KV_W, KEY_TILE), lambda b, i: (b, i, 0, 0)),
                   pl.BlockSpec((1, tm // KEY_TILE, B_V_W, KEY_TILE), lambda b, i: (b, i, 0, 0))],
        out_shape=[jax.ShapeDtypeStruct((bsz, AB_Q_W, rows), BF16),
                   jax.ShapeDtypeStruct((bsz, rows, A_KV_W), BF16),
                   jax.ShapeDtypeStruct((bsz, rows, B_QK_W), BF16),
                   jax.ShapeDtypeStruct((bsz, rows // KEY_TILE, A_KV_W, KEY_TILE), BF16),
                   jax.ShapeDtypeStruct((bsz, rows // KEY_TILE, B_V_W, KEY_TILE), BF16)],
        compiler_params=_params(2),
        name="ab_project",
    )(x, mod, g, wqt, wk, wvt, cf, sf, ct, s1, s2)


def _softmax_tile(s, m, l, valid=None):
    if valid is not None:
        s = jnp.where(valid, s, NEG_INF)
    m_new = jnp.maximum(m, jnp.max(s, axis=0, keepdims=True))
    p = jnp.exp2(s - m_new)
    alpha = jnp.exp2(m - m_new)
    l_new = alpha * l + jnp.sum(p, axis=0, keepdims=True)
    return p.astype(BF16), alpha, m_new, l_new


def _attend(chains, k_parts, v_parts, acc_ref, m0=None, window=None):
    n = len(chains)
    tq = chains[0][0].shape[1]
    acc_ref[...] = jnp.zeros(acc_ref.shape, F32)
    carry = []
    for c in range(n):
        if m0 is None:
            carry += [jnp.full((1, tq), NEG_INF, F32), jnp.zeros((1, tq), F32)]
        else:
            carry += [m0[c], jnp.ones((1, tq), F32)]
    carry = tuple(carry)

    def step(t, carry, k_ref, v_ref, valid_fn):
        start = t * KEY_TILE if isinstance(t, int) else pl.multiple_of(t * KEY_TILE, KEY_TILE)
        valid = None if valid_fn is None else valid_fn(t)
        logits = [_dot(k_ref[0, pl.ds(start, KEY_TILE), lanes], q_pad) for q_pad, lanes, _ in chains]
        tiles = [_softmax_tile(logits[c], carry[2 * c], carry[2 * c + 1], valid) for c in range(n)]
        out = []
        for c, (p, alpha, m_new, l_new) in enumerate(tiles):
            acc_ref[c] = alpha * acc_ref[c] + _dot(v_ref[0, t, chains[c][2], :], p)
            out += [m_new, l_new]
        return tuple(out)

    for part, (k_ref, v_ref) in enumerate(zip(k_parts, v_parts)):
        n_tiles = v_ref.shape[1]
        if part == 0 and window is not None:
            lo, hi, valid_fn = window
            carry = lax.fori_loop(lo, hi, functools.partial(step, k_ref=k_ref, v_ref=v_ref, valid_fn=valid_fn), carry)
        elif n_tiles == 1:
            carry = step(0, carry, k_ref, v_ref, None)
        else:
            carry = lax.fori_loop(0, n_tiles, functools.partial(step, k_ref=k_ref, v_ref=v_ref, valid_fn=None), carry)
    return [acc_ref[c] * (1.0 / carry[2 * c + 1]) for c in range(n)]


def _attend_pipelined(chains, k_parts, v_parts, acc_ref, s_ref):
    n = len(chains)
    tq = chains[0][0].shape[1]
    (k_lat, k_ctx), (v_lat, v_ctx) = k_parts, v_parts
    n_lat = v_lat.shape[1]
    assert v_ctx.shape[1] == 1 and n_lat % 2 == 0 and n_lat >= 4
    acc_ref[...] = jnp.zeros(acc_ref.shape, F32)

    def park(k_ref, t, slot):
        start = t * KEY_TILE if isinstance(t, int) else pl.multiple_of(t * KEY_TILE, KEY_TILE)
        maxima = []
        for c, (q_pad, lanes, _) in enumerate(chains):
            s = _dot(k_ref[0, pl.ds(start, KEY_TILE), lanes], q_pad)
            s_ref[slot, c] = s
            maxima.append(jnp.max(s, axis=0, keepdims=True))
        return maxima

    def finish(v_ref, t, slot, stats, maxima):
        out = []
        for c in range(n):
            m, l = stats[2 * c], stats[2 * c + 1]
            m_new = jnp.maximum(m, maxima[c])
            p = jnp.exp2(s_ref[slot, c] - m_new)
            alpha = jnp.exp2(m - m_new)
            out += [m_new, alpha * l + jnp.sum(p, axis=0, keepdims=True)]
            acc_ref[c] = alpha * acc_ref[c] + _dot(v_ref[0, t, chains[c][2], :], p.astype(BF16))
        return out

    def pair(j, carry):
        t = 2 * j
        stats, max_a = list(carry[:2 * n]), list(carry[2 * n:])
        max_b = park(k_lat, t + 1, 1)
        stats = finish(v_lat, t, 0, stats, max_a)
        max_a = park(k_lat, t + 2, 0)
        stats = finish(v_lat, t + 1, 1, stats, max_b)
        return tuple(stats + max_a)

    stats = []
    for c in range(n):
        stats += [jnp.full((1, tq), NEG_INF, F32), jnp.zeros((1, tq), F32)]
    carry = lax.fori_loop(0, n_lat // 2 - 1, pair, tuple(stats + park(k_lat, 0, 0)))
    stats, max_a = list(carry[:2 * n]), list(carry[2 * n:])
    max_b = park(k_lat, n_lat - 1, 1)
    stats = finish(v_lat, n_lat - 2, 0, stats, max_a)
    max_a = park(k_ctx, 0, 0)
    stats = finish(v_lat, n_lat - 1, 1, stats, max_b)
    stats = finish(v_ctx, 0, 0, stats, max_a)
    return [acc_ref[c] * (1.0 / stats[2 * c + 1]) for c in range(n)]


def _split_parts(refs, n_parts):
    return refs[:n_parts], refs[n_parts:2 * n_parts], refs[2 * n_parts:]


B_HEADS_PER_STEP = 2


def _diff_attn_kernel(lam_init, n_parts, qt_ref, *refs):
    k_parts, v_parts, (lam_ref, g_ref, o_ref, acc_ref, s_ref) = _split_parts(refs, n_parts)
    chains = []
    for j in range(B_HEADS_PER_STEP):
        q = qt_ref[0, 128 * j:128 * (j + 1), :]
        row = lax.broadcasted_iota(jnp.int32, q.shape, 0)
        zero = jnp.zeros_like(q)
        slab, vrows = slice(128 * j, 128 * (j + 1)), slice(B_V_DIM * j, B_V_DIM * (j + 1))
        chains.append((jnp.where(row < B_QK_DIM, q, zero), slab, vrows))
        chains.append((jnp.where(row >= B_QK_DIM, q, zero), slab, vrows))
    if n_parts == 2:
        outs = _attend_pipelined(chains, k_parts, v_parts, acc_ref, s_ref)
    else:
        outs = _attend(chains, k_parts, v_parts, acc_ref)
    lv = lam_ref[...]
    lam = (jnp.exp(jnp.sum(lv[0:1] * lv[1:2], axis=1, keepdims=True))
           - jnp.exp(jnp.sum(lv[2:3] * lv[3:4], axis=1, keepdims=True)) + lam_init)
    for j in range(B_HEADS_PER_STEP):
        dlt = outs[2 * j] - lam * outs[2 * j + 1]
        y = dlt * lax.rsqrt(jnp.mean(dlt * dlt, axis=0, keepdims=True) + NORM_EPS) * g_ref[...] * (1.0 - lam_init)
        o_ref[0, :, B_V_DIM * j:B_V_DIM * (j + 1)] = y.T.astype(BF16)


def _diff_attention(qt, k_parts, v_parts, lam_vec, subln_g, lam_init, tq):
    bsz, _, rows = qt.shape
    per = B_HEADS_PER_STEP
    q_off = A_Q_W // (128 * per)
    in_specs = [pl.BlockSpec((1, 128 * per, tq), lambda b, h, i: (b, q_off + h, i))]
    for k in k_parts:
        in_specs.append(pl.BlockSpec((1, k.shape[1], 128 * per), lambda b, h, i: (b, 0, h)))
    for v in v_parts:
        in_specs.append(pl.BlockSpec((1, v.shape[1], B_V_DIM * per, KEY_TILE), lambda b, h, i: (b, 0, h, 0)))
    in_specs += [pl.BlockSpec(lam_vec.shape, lambda b, h, i: (0, 0)),
                 pl.BlockSpec((B_V_DIM, 1), lambda b, h, i: (0, 0))]
    return pl.pallas_call(
        functools.partial(_diff_attn_kernel, lam_init, len(k_parts)),
        grid=(bsz, B_HEADS // per, rows // tq),
        in_specs=in_specs,
        out_specs=pl.BlockSpec((1, tq, B_V_DIM * per), lambda b, h, i: (b, i, h)),
        out_shape=jax.ShapeDtypeStruct((bsz, rows, B_V_W), BF16),
        scratch_shapes=[pltpu.VMEM((2 * per, B_V_DIM, tq), F32), pltpu.VMEM((2, 2 * per, KEY_TILE, tq), F32)],
        compiler_params=_params(3),
        name="diff_attention",
    )(qt, *k_parts, *v_parts, lam_vec, subln_g.reshape(B_V_DIM, 1))


def _window_attn_kernel(windowed, n_parts, sink_ref, qt_ref, *refs):
    k_parts, v_parts, (o_ref, acc_ref, slab_ref) = _split_parts(refs, n_parts)
    tq = qt_ref.shape[2]
    i = pl.program_id(1)
    group = A_HEADS // A_KV_HEADS
    zeros = jnp.zeros((A_HEAD_DIM, tq), BF16)
    window = None
    if windowed:
        per_q = tq // KEY_TILE
        lo = jnp.maximum(i * per_q - 1, 0)
        hi = jnp.minimum((i + 1) * per_q + 1, v_parts[0].shape[1])
        rel = (lax.broadcasted_iota(jnp.int32, (KEY_TILE, tq), 1)
               - lax.broadcasted_iota(jnp.int32, (KEY_TILE, tq), 0))
        window = (lo, hi, lambda t: jnp.abs(rel + (i * tq - t * KEY_TILE)) <= WINDOW)
    for kv in range(A_KV_HEADS):
        rows = slice(A_HEAD_DIM * kv, A_HEAD_DIM * (kv + 1))
        chains, m0 = [], []
        for hq in range(group * kv, group * (kv + 1)):
            qh = qt_ref[0, A_HEAD_DIM * hq:A_HEAD_DIM * (hq + 1), :]
            chains.append((jnp.concatenate([qh, zeros] if kv == 0 else [zeros, qh], axis=0), slice(None), rows))
            m0.append(jnp.full((1, tq), sink_ref[hq] * LOG2E, F32))
        outs = _attend(chains, k_parts, v_parts, acc_ref, m0=m0, window=window)
        for c in range(group):
            slab_ref[A_HEAD_DIM * c:A_HEAD_DIM * (c + 1), :] = outs[c]
        width = group * A_HEAD_DIM
        o_ref[0, :, width * kv:width * (kv + 1)] = slab_ref[...].T.astype(BF16)


def _window_attention(qt, k_parts, v_parts, sink, windowed, tq):
    bsz, _, rows = qt.shape
    in_specs = [pl.BlockSpec(memory_space=pltpu.SMEM),
                pl.BlockSpec((1, A_Q_W, tq), lambda b, i: (b, 0, i))]
    for k in k_parts:
        in_specs.append(pl.BlockSpec((1, k.shape[1], 128), lambda b, i: (b, 0, 0)))
    for v in v_parts:
        in_specs.append(pl.BlockSpec((1, v.shape[1], 128, KEY_TILE), lambda b, i: (b, 0, 0, 0)))
    return pl.pallas_call(
        functools.partial(_window_attn_kernel, windowed, len(k_parts)),
        grid=(bsz, rows // tq),
        in_specs=in_specs,
        out_specs=pl.BlockSpec((1, tq, A_Q_W), lambda b, i: (b, i, 0)),
        out_shape=jax.ShapeDtypeStruct((bsz, rows, A_Q_W), BF16),
        scratch_shapes=[pltpu.VMEM((A_HEADS // A_KV_HEADS, A_HEAD_DIM, tq), F32),
                        pltpu.VMEM((A_Q_W // A_KV_HEADS, tq), F32)],
        compiler_params=_params(2),
        name="window_attention",
    )(sink, qt, *k_parts, *v_parts)


def _mla_proj_kernel(with_q, x_ref, mod_ref, g_ref, win_ref, wkr_ref, qg_ref, kvg_ref, wqt_ref, wkn_ref, wvt_ref,
                     cf_ref, sf_ref, ct_ref, s1_ref, s2_ref, *out_refs):
    h = _rms(x_ref[0], g_ref[...]) * (1.0 + mod_ref[0, 1:2, :]) + mod_ref[0, 0:1, :]
    hb = h.astype(BF16)
    lat = _dot(hb, win_ref[...])
    kvn = _rms(lat[:, C_Q_LORA:], kvg_ref[...]).astype(BF16)
    if with_q:
        qt_ref, k_ref, vt_ref = out_refs
        qn = _rms(lat[:, :C_Q_LORA], qg_ref[...]).astype(BF16)
        qt = _dot_nt(wqt_ref[...], qn)
        cf, sf = cf_ref[...], sf_ref[...]
        scale = LOG2E * (C_NOPE + C_ROPE) ** -0.5
        for hd in range(C_HEADS):
            base = C_SLAB * hd
            rope = _rope_rows(qt[base + C_NOPE:base + C_NOPE + C_ROPE], C_ROPE // 4, cf, sf)
            blk = jnp.concatenate([qt[base:base + C_NOPE], rope, qt[base + C_NOPE + C_ROPE:base + C_SLAB]], axis=0)
            qt_ref[0, base:base + C_SLAB, :] = (blk * scale).astype(BF16)
    else:
        k_ref, vt_ref = out_refs
    kn = _dot(kvn, wkn_ref[...])
    kr = _rope_lanes(_dot(hb, wkr_ref[...]), C_ROPE // 4, ct_ref[...], s1_ref[...], s2_ref[...])
    for hd in range(C_HEADS):
        k_ref[0, :, C_SLAB * hd:C_SLAB * (hd + 1)] = (kn[:, C_SLAB * hd:C_SLAB * (hd + 1)] + kr).astype(BF16)
    vt = _dot_nt(wvt_ref[...], kvn)
    for j in range(vt.shape[1] // KEY_TILE):
        vt_ref[0, j] = vt[:, KEY_TILE * j:KEY_TILE * j + KEY_TILE].astype(BF16)


def _mla_project(x, mod, g, weights, tabs, with_q, tm):
    bsz, rows, d = x.shape
    win, wkr, qg, kvg, wqt, wkn, wvt = weights
    cf, sf, ct, s1, s2 = tabs
    mod_map = (lambda b, i: (b, 0, 0)) if mod.shape[0] == bsz else (lambda b, i: (0, 0, 0))
    kw, vw = C_HEADS * C_SLAB, C_HEADS * C_V
    out_specs = [pl.BlockSpec((1, tm, kw), lambda b, i: (b, i, 0)),
                 pl.BlockSpec((1, tm // KEY_TILE, vw, KEY_TILE), lambda b, i: (b, i, 0, 0))]
    out_shape = [jax.ShapeDtypeStruct((bsz, rows, kw), BF16),
                 jax.ShapeDtypeStruct((bsz, rows // KEY_TILE, vw, KEY_TILE), BF16)]
    if with_q:
        out_specs.insert(0, pl.BlockSpec((1, kw, tm), lambda b, i: (b, 0, i)))
        out_shape.insert(0, jax.ShapeDtypeStruct((bsz, kw, rows), BF16))
    return pl.pallas_call(
        functools.partial(_mla_proj_kernel, with_q),
        grid=(bsz, rows // tm),
        in_specs=[pl.BlockSpec((1, tm, d), lambda b, i: (b, i, 0)),
                  pl.BlockSpec((1, 6, d), mod_map),
                  _const_spec((1, d)), _const_spec(win.shape), _const_spec(wkr.shape),
                  _const_spec(qg.shape), _const_spec(kvg.shape),
                  _const_spec(wqt.shape), _const_spec(wkn.shape), _const_spec(wvt.shape),
                  pl.BlockSpec((16, tm), lambda b, i: (0, i)),
                  pl.BlockSpec((16, tm), lambda b, i: (0, i)),
                  pl.BlockSpec((tm, 128), lambda b, i: (i, 0)),
                  pl.BlockSpec((tm, 128), lambda b, i: (i, 0)),
                  pl.BlockSpec((tm, 128), lambda b, i: (i, 0))],
        out_specs=out_specs,
        out_shape=out_shape,
        compiler_params=_params(2),
        name="mla_project",
    )(x, mod, g, win, wkr, qg, kvg, wqt, wkn, wvt, cf, sf, ct, s1, s2)


C_HEADS_PER_STEP = 4


def _mla_attn_kernel(n_parts, qt_ref, *refs):
    k_parts, v_parts, (o_ref, acc_ref, slab_ref, s_ref) = _split_parts(refs, n_parts)
    chains = [(qt_ref[0, C_SLAB * j:C_SLAB * (j + 1), :], slice(C_SLAB * j, C_SLAB * (j + 1)),
               slice(C_V * j, C_V * (j + 1))) for j in range(C_HEADS_PER_STEP)]
    if n_parts == 2:
        outs = _attend_pipelined(chains, k_parts, v_parts, acc_ref, s_ref)
    else:
        outs = _attend(chains, k_parts, v_parts, acc_ref)
    for j in range(C_HEADS_PER_STEP):
        slab_ref[C_V * j:C_V * (j + 1), :] = outs[j]
    o_ref[0] = slab_ref[...].T.astype(BF16)


def _mla_attention(qt, k_parts, v_parts, tq):
    bsz, _, rows = qt.shape
    per = C_HEADS_PER_STEP
    in_specs = [pl.BlockSpec((1, per * C_SLAB, tq), lambda b, p, i: (b, p, i))]
    for k in k_parts:
        in_specs.append(pl.BlockSpec((1, k.shape[1], per * C_SLAB), lambda b, p, i: (b, 0, p)))
    for v in v_parts:
        in_specs.append(pl.BlockSpec((1, v.shape[1], per * C_V, KEY_TILE), lambda b, p, i: (b, 0, p, 0)))
    return pl.pallas_call(
        functools.partial(_mla_attn_kernel, len(k_parts)),
        grid=(bsz, C_HEADS // per, rows // tq),
        in_specs=in_specs,
        out_specs=pl.BlockSpec((1, tq, per * C_V), lambda b, p, i: (b, i, p)),
        out_shape=jax.ShapeDtypeStruct((bsz, rows, C_HEADS * C_V), BF16),
        scratch_shapes=[pltpu.VMEM((per, C_V, tq), F32), pltpu.VMEM((per * C_V, tq), F32),
                        pltpu.VMEM((2, per, KEY_TILE, tq), F32)],
        compiler_params=_params(3),
        name="mla_attention",
    )(qt, *k_parts, *v_parts)


def _mix_ffn_kernel(n_o, *refs):
    o_refs = refs[:n_o]
    x_ref, mod_ref, g_ref, wout_ref, w13_ref, w2_ref, out_ref = refs[n_o:]
    y = None
    row = 0
    for o_ref in o_refs:
        width = o_ref.shape[2]
        part = _dot(o_ref[0], wout_ref[row:row + width, :])
        y = part if y is None else y + part
        row += width
    x1 = x_ref[0] + mod_ref[0, 2:3, :] * _rms(y, g_ref[1:2, :])
    h = _rms(x1, g_ref[2:3, :]) * (1.0 + mod_ref[0, 4:5, :]) + mod_ref[0, 3:4, :]
    hb = h.astype(BF16)
    f = None
    for c in range(FFN_HIDDEN // FFN_CHUNK):
        lo = FFN_CHUNK * c
        gate = _dot(hb, w13_ref[:, lo:lo + FFN_CHUNK])
        up = _dot(hb, w13_ref[:, FFN_HIDDEN + lo:FFN_HIDDEN + lo + FFN_CHUNK])
        act = (gate / (1.0 + jnp.exp(-gate)) * up).astype(BF16)
        part = _dot(act, w2_ref[lo:lo + FFN_CHUNK, :])
        f = part if f is None else f + part
    out_ref[0] = x1 + mod_ref[0, 5:6, :] * _rms(f, g_ref[3:4, :])


def _mix_ffn(o_parts, x, mod, g4, wout, w13, w2, tm):
    bsz, rows, d = x.shape
    mod_map = (lambda b, i: (b, 0, 0)) if mod.shape[0] == bsz else (lambda b, i: (0, 0, 0))
    in_specs = [pl.BlockSpec((1, tm, o.shape[2]), lambda b, i: (b, i, 0)) for o in o_parts]
    in_specs += [pl.BlockSpec((1, tm, d), lambda b, i: (b, i, 0)),
                 pl.BlockSpec((1, 6, d), mod_map),
                 _const_spec(g4.shape), _const_spec(wout.shape), _const_spec(w13.shape), _const_spec(w2.shape)]
    return pl.pallas_call(
        functools.partial(_mix_ffn_kernel, len(o_parts)),
        grid=(bsz, rows // tm),
        in_specs=in_specs,
        out_specs=pl.BlockSpec((1, tm, d), lambda b, i: (b, i, 0)),
        out_shape=jax.ShapeDtypeStruct((bsz, rows, d), F32),
        compiler_params=_params(2),
        name="mix_ffn",
    )(*o_parts, x, mod, g4, wout, w13, w2)


def _rope_tables(n, rot_dim):
    pos = jnp.arange(n, dtype=jnp.int32)
    row = (pos // GRID_W).astype(F32)
    col = (pos % GRID_W).astype(F32)
    axis_dim = rot_dim // 2
    inv_freq = ROPE_THETA ** (-jnp.arange(0, axis_dim, 2, dtype=F32) / axis_dim)
    ang = jnp.concatenate([row[:, None] * inv_freq, col[:, None] * inv_freq], axis=-1)
    return jnp.cos(ang), jnp.sin(ang)


def _lane_tables(cos, sin, lane_dims, quarter):
    dims = jnp.asarray(lane_dims, dtype=jnp.int32)
    live = dims >= 0
    safe = jnp.where(live, dims, 0)
    axis, within = safe // (2 * quarter), safe % (2 * quarter)
    idx = axis * quarter + within % quarter
    first = within < quarter
    c = jnp.where(live[None, :], cos[:, idx], 1.0)
    s = jnp.where(live[None, :], sin[:, idx], 0.0)
    return c, jnp.where(first[None, :], -s, 0.0), jnp.where(first[None, :], 0.0, s)


def _identity_tables(n, half):
    return (jnp.ones((half, n), F32), jnp.zeros((half, n), F32),
            jnp.ones((n, 128), F32), jnp.zeros((n, 128), F32), jnp.zeros((n, 128), F32))


def kernel(x, c, ctx, c_ctx, ada_w, ada_b, norm_g, ffn_w13, ffn_w2, ab_w_in, ab_w_out, ab_sink, diff_lambda,
           diff_subln_g, mla_w_in, mla_q_norm_g, mla_kv_norm_g, mla_wq_b, mla_wkv_b, mla_w_out):
    bsz, n, d = x.shape
    n_ctx = ctx.shape[1]
    depth = ada_w.shape[0]
    assert n_ctx == KEY_TILE and n % Q_TILE == 0 and n % ROW_TILE == 0 and bsz + 1 <= 16

    cond = jnp.zeros((16, d), F32).at[:bsz].set(c).at[bsz].set(c_ctx)
    mods = _ada_modulation(cond, ada_w, ada_b).reshape(depth, 16, 6, d)

    cos_ab, sin_ab = _rope_tables(n, A_HEAD_DIM)
    ab_lane = [l % 64 for l in range(128)]
    tabs_ab = (cos_ab.T, sin_ab.T) + _lane_tables(cos_ab, sin_ab, ab_lane, 16)
    cos_c, sin_c = _rope_tables(n, C_ROPE)
    c_lane = [l - C_NOPE if C_NOPE <= l < C_NOPE + C_ROPE else -1 for l in range(128)]
    tabs_c = (cos_c.T, sin_c.T) + _lane_tables(cos_c, sin_c, c_lane, C_ROPE // 4)

    for l in range(depth):
        last = l == depth - 1
        mod_x, mod_c = mods[l, :bsz], mods[l, bsz:bsz + 1]
        g_in = norm_g[l, 0:1]
        w13, w2 = ffn_w13[l].astype(BF16), ffn_w2[l].astype(BF16)
        if l % 2 == 0:
            e = l // 2
            w = ab_w_in[e]
            o1, o2 = AB_Q_W + A_KV_W, AB_Q_W + 2 * A_KV_W
            o3 = o2 + B_QK_W
            wqt = w[:, :AB_Q_W].T.astype(BF16)
            wk = jnp.concatenate([w[:, AB_Q_W:o1], w[:, o2:o3]], axis=1).astype(BF16)
            wvt = jnp.concatenate([w[:, o1:o2], w[:, o3:]], axis=1).T.astype(BF16)
            wout = ab_w_out[e].astype(BF16)
            lam_init = 0.8 - 0.6 * math.exp(-0.3 * l)
            qt_x, ka_x, kb_x, va_x, vb_x = _ab_project(x, mod_x, g_in, wqt, wk, wvt, tabs_ab, ROW_TILE)
            qt_c, ka_c, kb_c, va_c, vb_c = _ab_project(ctx, mod_c, g_in, wqt, wk, wvt,
                                                       _identity_tables(n_ctx, 32), n_ctx)
            lam_args = (diff_lambda[e], diff_subln_g[e], lam_init)
            oa = _window_attention(qt_x, [ka_x, ka_c], [va_x, va_c], ab_sink[e], True, Q_TILE)
            ob = _diff_attention(qt_x, [kb_x, kb_c], [vb_x, vb_c], *lam_args, Q_TILE)
            o_x = [oa, ob]
            if not last:
                oca = _window_attention(qt_c, [ka_c], [va_c], ab_sink[e], False, n_ctx)
                ocb = _diff_attention(qt_c, [kb_c], [vb_c], *lam_args, n_ctx)
                o_c = [oca, ocb]
        else:
            o = l // 2
            w = mla_w_in[o]
            win = w[:, :C_LAT_W].astype(BF16)
            wkr = jnp.zeros((d, C_SLAB), F32).at[:, C_NOPE:C_NOPE + C_ROPE].set(w[:, C_LAT_W:]).astype(BF16)
            wq = mla_wq_b[o].reshape(C_Q_LORA, C_HEADS, C_NOPE + C_ROPE)
            wq = jnp.pad(wq, ((0, 0), (0, 0), (0, C_SLAB - C_NOPE - C_ROPE)))
            wqt = wq.reshape(C_Q_LORA, C_HEADS * C_SLAB).T.astype(BF16)
            wkv = mla_wkv_b[o].reshape(C_KV_LORA, C_HEADS, C_NOPE + C_V)
            wkn = jnp.pad(wkv[:, :, :C_NOPE], ((0, 0), (0, 0), (0, C_SLAB - C_NOPE)))
            wkn = wkn.reshape(C_KV_LORA, C_HEADS * C_SLAB).astype(BF16)
            wvt = wkv[:, :, C_NOPE:].reshape(C_KV_LORA, C_HEADS * C_V).T.astype(BF16)
            weights = (win, wkr, mla_q_norm_g[o:o + 1], mla_kv_norm_g[o:o + 1], wqt, wkn, wvt)
            wout = mla_w_out[o].astype(BF16)
            qt_x, k_x, vt_x = _mla_project(x, mod_x, g_in, weights, tabs_c, True, ROW_TILE)
            ident = _identity_tables(n_ctx, 16)
            if last:
                k_c, vt_c = _mla_project(ctx, mod_c, g_in, weights, ident, False, n_ctx)
            else:
                qt_c, k_c, vt_c = _mla_project(ctx, mod_c, g_in, weights, ident, True, n_ctx)
            o_x = [_mla_attention(qt_x, [k_x, k_c], [vt_x, vt_c], Q_TILE)]
            if not last:
                o_c = [_mla_attention(qt_c, [k_c], [vt_c], n_ctx)]
        x = _mix_ffn(o_x, x, mod_x, norm_g[l], wout, w13, w2, ROW_TILE)
        if not last:
            ctx = _mix_ffn(o_c, ctx, mod_c, norm_g[l], wout, w13, w2, n_ctx)
    return x
```

```python
import functools
import math

import jax
import jax.numpy as jnp
from jax import lax
from jax.experimental import pallas as pl
from jax.experimental.pallas import tpu as pltpu

F32 = jnp.float32
BF16 = jnp.bfloat16

D_MODEL = 1024
GRID_W = 64
ROPE_THETA = 10000.0
NORM_EPS = 1e-6
NEG_INF = -1e30
WINDOW = 128
LOG2E = 1.4426950408889634

A_HEADS, A_KV_HEADS, A_HEAD_DIM = 8, 2, 64
B_HEADS, B_QK_DIM, B_V_DIM = 4, 64, 128
A_Q_W = A_HEADS * A_HEAD_DIM
A_KV_W = A_KV_HEADS * A_HEAD_DIM
B_QK_W = B_HEADS * 2 * B_QK_DIM
B_V_W = B_HEADS * B_V_DIM
AB_Q_W = A_Q_W + B_QK_W
AB_K_W = A_KV_W + B_QK_W
AB_V_W = A_KV_W + B_V_W

C_HEADS, C_Q_LORA, C_KV_LORA, C_NOPE, C_ROPE, C_V = 16, 384, 256, 64, 32, 64
C_SLAB = 128
C_LAT_W = C_Q_LORA + C_KV_LORA

FFN_HIDDEN = 2816
FFN_CHUNK = 256

ONES_ROWS = 16
B_V_AUG = B_V_DIM + ONES_ROWS
C_V_AUG = C_V + ONES_ROWS

KEY_TILE = 256
LAT_KEY_TILE = 512
PIPE_AHEAD = 2
PIPE_SLOTS = PIPE_AHEAD + 1
Q_TILE = 512
ROW_TILE = 512
VMEM_LIMIT = 56 * 1024 * 1024

_NT = (((1,), (1,)), ((), ()))


def _dot(a, b):
    return jnp.dot(a, b, preferred_element_type=F32)


def _dot_nt(a, b):
    return lax.dot_general(a, b, _NT, preferred_element_type=F32)


def _rms(x, g):
    return x * lax.rsqrt(jnp.mean(x * x, axis=-1, keepdims=True) + NORM_EPS) * g


def _const_spec(shape):
    zeros = (0,) * len(shape)
    return pl.BlockSpec(shape, lambda *_: zeros, pipeline_mode=pl.Buffered(1))


def _params(n_axes):
    return pltpu.CompilerParams(dimension_semantics=("arbitrary",) * n_axes,
                                vmem_limit_bytes=VMEM_LIMIT)


def _ada_kernel(c_ref, w_ref, b_ref, o_ref):
    c = c_ref[...]
    s = c / (1.0 + jnp.exp(-c))
    o_ref[0] = _dot(s.astype(BF16), w_ref[0].astype(BF16)) + b_ref[0]


def _ada_modulation(cond, ada_w, ada_b):
    depth, d, n = ada_w.shape
    tn = 1536
    return pl.pallas_call(
        _ada_kernel,
        grid=(depth, n // tn),
        in_specs=[pl.BlockSpec((16, d), lambda l, j: (0, 0)),
                  pl.BlockSpec((1, d, tn), lambda l, j: (l, 0, j)),
                  pl.BlockSpec((1, 1, tn), lambda l, j: (l, 0, j))],
        out_specs=pl.BlockSpec((1, 16, tn), lambda l, j: (l, 0, j)),
        out_shape=jax.ShapeDtypeStruct((depth, 16, n), F32),
        compiler_params=_params(2),
        name="ada_modulation",
    )(cond, ada_w, ada_b.reshape(depth, 1, n))


def _rope_rows(blk, quarter, cf, sf):
    a, b = blk[0:quarter], blk[quarter:2 * quarter]
    c, d = blk[2 * quarter:3 * quarter], blk[3 * quarter:4 * quarter]
    cr, sr = cf[0:quarter], sf[0:quarter]
    cc, sc = cf[quarter:2 * quarter], sf[quarter:2 * quarter]
    return jnp.concatenate([a * cr - b * sr, b * cr + a * sr, c * cc - d * sc, d * cc + c * sc], axis=0)


def _rope_lanes(x, quarter, ct, s1, s2):
    return x * ct + pltpu.roll(x, 128 - quarter, 1) * s1 + pltpu.roll(x, quarter, 1) * s2


def _store_with_ones(vt_ref, tile, heads, dv):
    row = lax.broadcasted_iota(jnp.int32, (ONES_ROWS, tile.shape[1]), 0)
    ones_blk = jnp.where(row == 0, 1.0, 0.0).astype(BF16)
    aug = dv + ONES_ROWS
    for h in range(heads):
        vt_ref[aug * h:aug * h + dv, :] = tile[dv * h:dv * (h + 1)]
        vt_ref[aug * h + dv:aug * (h + 1), :] = ones_blk


def _ab_proj_kernel(x_ref, mod_ref, g_ref, wqt_ref, wk_ref, wvt_ref, cf_ref, sf_ref, ct_ref, s1_ref, s2_ref,
                    qt_ref, ka_ref, kb_ref, vta_ref, vtb_ref):
    h = _rms(x_ref[0], g_ref[...]) * (1.0 + mod_ref[0, 1:2, :]) + mod_ref[0, 0:1, :]
    hb = h.astype(BF16)
    cf, sf = cf_ref[...], sf_ref[...]
    qt = _dot_nt(wqt_ref[...], hb)
    for g in range(AB_Q_W // 64):
        blk = _rope_rows(qt[64 * g:64 * g + 64], 16, cf, sf)
        qt_ref[0, 64 * g:64 * g + 64, :] = (blk * (LOG2E * 64 ** -0.5)).astype(BF16)
    kk = _dot(hb, wk_ref[...])
    ct, s1, s2 = ct_ref[...], s1_ref[...], s2_ref[...]
    ka_ref[0] = _rope_lanes(kk[:, :A_KV_W], 16, ct, s1, s2).astype(BF16)
    for c in range(B_QK_W // 128):
        lo = A_KV_W + 128 * c
        kb_ref[0, :, 128 * c:128 * c + 128] = _rope_lanes(kk[:, lo:lo + 128], 16, ct, s1, s2).astype(BF16)
    vt = _dot_nt(wvt_ref[...], hb).astype(BF16)
    tka, tkb = vta_ref.shape[3], vtb_ref.shape[3]
    for j in range(vt.shape[1] // tka):
        vta_ref[0, j] = vt[:A_KV_W, tka * j:tka * (j + 1)]
    for j in range(vt.shape[1] // tkb):
        _store_with_ones(vtb_ref.at[0, j], vt[A_KV_W:, tkb * j:tkb * (j + 1)], B_HEADS, B_V_DIM)


def _ab_project(x, mod, g, wqt, wk, wvt, tabs, tm):
    bsz, rows, d = x.shape
    cf, sf, ct, s1, s2 = tabs
    mod_map = (lambda b, i: (b, 0, 0)) if mod.shape[0] == bsz else (lambda b, i: (0, 0, 0))
    tkb = min(tm, LAT_KEY_TILE)
    return pl.pallas_call(
        _ab_proj_kernel,
        grid=(bsz, rows // tm),
        in_specs=[pl.BlockSpec((1, tm, d), lambda b, i: (b, i, 0)),
                  pl.BlockSpec((1, 6, d), mod_map),
                  _const_spec((1, d)), _const_spec(wqt.shape), _const_spec(wk.shape), _const_spec(wvt.shape),
                  pl.BlockSpec((32, tm), lambda b, i: (0, i)),
                  pl.BlockSpec((32, tm), lambda b, i: (0, i)),
                  pl.BlockSpec((tm, 128), lambda b, i: (i, 0)),
                  pl.BlockSpec((tm, 128), lambda b, i: (i, 0)),
                  pl.BlockSpec((tm, 128), lambda b, i: (i, 0))],
        out_specs=[pl.BlockSpec((1, AB_Q_W, tm), lambda b, i: (b, 0, i)),
                   pl.BlockSpec((1, tm, A_KV_W), lambda b, i: (b, i, 0)),
                   pl.BlockSpec((1, tm, B_QK_W), lambda b, i: (b, i, 0)),
                   pl.BlockSpec((1, tm // KEY_TILE, A_KV_W, KEY_TILE), lambda b, i: (b, i, 0, 0)),
                   pl.BlockSpec((1, tm // tkb, B_HEADS * B_V_AUG, tkb), lambda b, i: (b, i, 0, 0))],
        out_shape=[jax.ShapeDtypeStruct((bsz, AB_Q_W, rows), BF16),
                   jax.ShapeDtypeStruct((bsz, rows, A_KV_W), BF16),
                   jax.ShapeDtypeStruct((bsz, rows, B_QK_W), BF16),
                   jax.ShapeDtypeStruct((bsz, rows // KEY_TILE, A_KV_W, KEY_TILE), BF16),
                   jax.ShapeDtypeStruct((bsz, rows // tkb, B_HEADS * B_V_AUG, tkb), BF16)],
        compiler_params=_params(2),
        name="ab_project",
    )(x, mod, g, wqt, wk, wvt, cf, sf, ct, s1, s2)


def _tile_start(t, size):
    return t * size if isinstance(t, int) else pl.multiple_of(t * size, size)


def _normalise(acc, dv, l=None):
    return acc[:dv] * (1.0 / (acc[dv:dv + 1] if l is None else l))


def _attend(chains, k_parts, v_parts, acc_ref, dv, m0=None, window=None):
    n = len(chains)
    tq = chains[0][0].shape[1]
    ones_row = acc_ref.shape[1] > dv
    assert not (ones_row and m0 is not None)
    acc_ref[...] = jnp.zeros(acc_ref.shape, F32)
    carry = []
    for c in range(n):
        carry.append(jnp.full((1, tq), NEG_INF, F32) if m0 is None else m0[c])
        if not ones_row:
            carry.append(jnp.zeros((1, tq), F32) if m0 is None else jnp.ones((1, tq), F32))
    per = len(carry) // n

    def step(t, carry, k_ref, v_ref, valid_fn):
        tk = v_ref.shape[3]
        start = _tile_start(t, tk)
        valid = None if valid_fn is None else valid_fn(t)
        logits = [_dot(k_ref[0, pl.ds(start, tk), lanes], q_pad) for q_pad, lanes, _ in chains]
        out, updates = [], []
        for c in range(n):
            s = logits[c] if valid is None else jnp.where(valid, logits[c], NEG_INF)
            m = carry[per * c]
            m_new = jnp.maximum(m, jnp.max(s, axis=0, keepdims=True))
            p = jnp.exp2(s - m_new)
            alpha = jnp.exp2(m - m_new)
            out.append(m_new)
            if not ones_row:
                out.append(alpha * carry[per * c + 1] + jnp.sum(p, axis=0, keepdims=True))
            updates.append((alpha, p.astype(BF16)))
        for c, (alpha, p) in enumerate(updates):
            acc_ref[c] = alpha * acc_ref[c] + _dot(v_ref[0, t, chains[c][2], :], p)
        return tuple(out)

    carry = tuple(carry)
    for part, (k_ref, v_ref) in enumerate(zip(k_parts, v_parts)):
        n_tiles = v_ref.shape[1]
        if part == 0 and window is not None:
            lo, hi, valid_fn = window
            carry = lax.fori_loop(lo, hi, functools.partial(step, k_ref=k_ref, v_ref=v_ref, valid_fn=valid_fn), carry)
        elif n_tiles == 1:
            carry = step(0, carry, k_ref, v_ref, None)
        else:
            carry = lax.fori_loop(0, n_tiles, functools.partial(step, k_ref=k_ref, v_ref=v_ref, valid_fn=None), carry)
    return [_normalise(acc_ref[c], dv, None if ones_row else carry[per * c + 1]) for c in range(n)]


def _attend_pipelined(chains, k_parts, v_parts, acc_ref, s_ref, dv):
    n = len(chains)
    tq = chains[0][0].shape[1]
    (k_lat, k_ctx), (v_lat, v_ctx) = k_parts, v_parts
    n_lat = v_lat.shape[1]
    assert v_ctx.shape[1] == 1 and n_lat >= PIPE_AHEAD and acc_ref.shape[1] > dv and s_ref.shape[0] == PIPE_SLOTS
    acc_ref[...] = jnp.zeros(acc_ref.shape, F32)

    def park(k_ref, v_ref, t, slot, c):
        tk = v_ref.shape[3]
        q_pad, lanes, _ = chains[c]
        s = _dot(k_ref[0, pl.ds(_tile_start(t, tk), tk), lanes], q_pad)
        s_ref[slot, c, 0:tk, :] = s
        return jnp.max(s, axis=0, keepdims=True)

    def finish(v_ref, t, slot, c, m, tile_max):
        tk = v_ref.shape[3]
        m_new = jnp.maximum(m, tile_max)
        p = jnp.exp2(s_ref[slot, c, 0:tk, :] - m_new)
        acc_ref[c] = jnp.exp2(m - m_new) * acc_ref[c] + _dot(v_ref[0, t, chains[c][2], :], p.astype(BF16))
        return m_new

    def sweep(t, slot, ms, maxima, ahead):
        for c in range(n):
            if ahead is not None:
                k_ref, v_ref, t_ahead = ahead
                maxima[(slot + PIPE_AHEAD) % PIPE_SLOTS][c] = park(k_ref, v_ref, t_ahead,
                                                                  (slot + PIPE_AHEAD) % PIPE_SLOTS, c)
            v_ref, t_cur = (v_lat, t) if t is not None else (v_ctx, 0)
            ms[c] = finish(v_ref, t_cur, slot, c, ms[c], maxima[slot][c])

    def unpack(carry):
        return list(carry[:n]), [list(carry[n * (1 + s):n * (2 + s)]) for s in range(PIPE_SLOTS)]

    def group(j, carry):
        ms, maxima = unpack(carry)
        for u in range(PIPE_SLOTS):
            t = PIPE_SLOTS * j + u
            sweep(t, u, ms, maxima, (k_lat, v_lat, t + PIPE_AHEAD))
        return tuple(ms + sum(maxima, []))

    ms = [jnp.full((1, tq), NEG_INF, F32) for _ in range(n)]
    maxima = [[jnp.full((1, tq), NEG_INF, F32) for _ in range(n)] for _ in range(PIPE_SLOTS)]
    for t in range(PIPE_AHEAD):
        maxima[t] = [park(k_lat, v_lat, t, t, c) for c in range(n)]
    n_groups = (n_lat - PIPE_AHEAD) // PIPE_SLOTS
    ms, maxima = unpack(lax.fori_loop(0, n_groups, group, tuple(ms + sum(maxima, []))))
    for t in range(PIPE_SLOTS * n_groups, n_lat + 1):
        t_ahead = t + PIPE_AHEAD
        ahead = None if t_ahead > n_lat else ((k_lat, v_lat, t_ahead) if t_ahead < n_lat else (k_ctx, v_ctx, 0))
        sweep(t if t < n_lat else None, t % PIPE_SLOTS, ms, maxima, ahead)
    return [_normalise(acc_ref[c], dv) for c in range(n)]


def _split_parts(refs, n_parts):
    return refs[:n_parts], refs[n_parts:2 * n_parts], refs[2 * n_parts:]


B_HEADS_PER_STEP = 2


def _diff_attn_kernel(lam_init, n_parts, qt_ref, *refs):
    k_parts, v_parts, (lam_ref, g_ref, o_ref, acc_ref, s_ref) = _split_parts(refs, n_parts)
    chains = []
    for j in range(B_HEADS_PER_STEP):
        q = qt_ref[0, 128 * j:128 * (j + 1), :]
        row = lax.broadcasted_iota(jnp.int32, q.shape, 0)
        zero = jnp.zeros_like(q)
        slab, vrows = slice(128 * j, 128 * (j + 1)), slice(B_V_AUG * j, B_V_AUG * (j + 1))
        chains.append((jnp.where(row < B_QK_DIM, q, zero), slab, vrows))
        chains.append((jnp.where(row >= B_QK_DIM, q, zero), slab, vrows))
    if n_parts == 2:
        outs = _attend_pipelined(chains, k_parts, v_parts, acc_ref, s_ref, B_V_DIM)
    else:
        outs = _attend(chains, k_parts, v_parts, acc_ref, B_V_DIM)
    lv = lam_ref[...]
    lam = (jnp.exp(jnp.sum(lv[0:1] * lv[1:2], axis=1, keepdims=True))
           - jnp.exp(jnp.sum(lv[2:3] * lv[3:4], axis=1, keepdims=True)) + lam_init)
    for j in range(B_HEADS_PER_STEP):
        dlt = outs[2 * j] - lam * outs[2 * j + 1]
        y = dlt * lax.rsqrt(jnp.mean(dlt * dlt, axis=0, keepdims=True) + NORM_EPS) * g_ref[...] * (1.0 - lam_init)
        o_ref[0, :, B_V_DIM * j:B_V_DIM * (j + 1)] = y.T.astype(BF16)


def _diff_attention(qt, k_parts, v_parts, lam_vec, subln_g, lam_init, tq):
    bsz, _, rows = qt.shape
    per = B_HEADS_PER_STEP
    q_off = A_Q_W // (128 * per)
    in_specs = [pl.BlockSpec((1, 128 * per, tq), lambda b, h, i: (b, q_off + h, i))]
    for k in k_parts:
        in_specs.append(pl.BlockSpec((1, k.shape[1], 128 * per), lambda b, h, i: (b, 0, h)))
    for v in v_parts:
        in_specs.append(pl.BlockSpec((1, v.shape[1], B_V_AUG * per, v.shape[3]), lambda b, h, i: (b, 0, h, 0)))
    in_specs += [pl.BlockSpec(lam_vec.shape, lambda b, h, i: (0, 0)),
                 pl.BlockSpec((B_V_DIM, 1), lambda b, h, i: (0, 0))]
    return pl.pallas_call(
        functools.partial(_diff_attn_kernel, lam_init, len(k_parts)),
        grid=(bsz, B_HEADS // per, rows // tq),
        in_specs=in_specs,
        out_specs=pl.BlockSpec((1, tq, B_V_DIM * per), lambda b, h, i: (b, i, h)),
        out_shape=jax.ShapeDtypeStruct((bsz, rows, B_V_W), BF16),
        scratch_shapes=[pltpu.VMEM((2 * per, B_V_AUG, tq), F32),
                        pltpu.VMEM((PIPE_SLOTS, 2 * per, v_parts[0].shape[3], tq), F32)],
        compiler_params=_params(3),
        name="diff_attention",
    )(qt, *k_parts, *v_parts, lam_vec, subln_g.reshape(B_V_DIM, 1))


def _window_attn_kernel(windowed, n_parts, sink_ref, qt_ref, *refs):
    k_parts, v_parts, (o_ref, acc_ref, slab_ref) = _split_parts(refs, n_parts)
    tq = qt_ref.shape[2]
    i = pl.program_id(1)
    group = A_HEADS // A_KV_HEADS
    zeros = jnp.zeros((A_HEAD_DIM, tq), BF16)
    window = None
    if windowed:
        per_q = tq // KEY_TILE
        lo = jnp.maximum(i * per_q - 1, 0)
        hi = jnp.minimum((i + 1) * per_q + 1, v_parts[0].shape[1])
        rel = (lax.broadcasted_iota(jnp.int32, (KEY_TILE, tq), 1)
               - lax.broadcasted_iota(jnp.int32, (KEY_TILE, tq), 0))
        window = (lo, hi, lambda t: jnp.abs(rel + (i * tq - t * KEY_TILE)) <= WINDOW)
    for kv in range(A_KV_HEADS):
        rows = slice(A_HEAD_DIM * kv, A_HEAD_DIM * (kv + 1))
        chains, m0 = [], []
        for hq in range(group * kv, group * (kv + 1)):
            qh = qt_ref[0, A_HEAD_DIM * hq:A_HEAD_DIM * (hq + 1), :]
            chains.append((jnp.concatenate([qh, zeros] if kv == 0 else [zeros, qh], axis=0), slice(None), rows))
            m0.append(jnp.full((1, tq), sink_ref[hq] * LOG2E, F32))
        outs = _attend(chains, k_parts, v_parts, acc_ref, A_HEAD_DIM, m0=m0, window=window)
        for c in range(group):
            slab_ref[A_HEAD_DIM * c:A_HEAD_DIM * (c + 1), :] = outs[c]
        width = group * A_HEAD_DIM
        o_ref[0, :, width * kv:width * (kv + 1)] = slab_ref[...].T.astype(BF16)


def _window_attention(qt, k_parts, v_parts, sink, windowed, tq):
    bsz, _, rows = qt.shape
    in_specs = [pl.BlockSpec(memory_space=pltpu.SMEM),
                pl.BlockSpec((1, A_Q_W, tq), lambda b, i: (b, 0, i))]
    for k in k_parts:
        in_specs.append(pl.BlockSpec((1, k.shape[1], 128), lambda b, i: (b, 0, 0)))
    for v in v_parts:
        in_specs.append(pl.BlockSpec((1, v.shape[1], 128, KEY_TILE), lambda b, i: (b, 0, 0, 0)))
    return pl.pallas_call(
        functools.partial(_window_attn_kernel, windowed, len(k_parts)),
        grid=(bsz, rows // tq),
        in_specs=in_specs,
        out_specs=pl.BlockSpec((1, tq, A_Q_W), lambda b, i: (b, i, 0)),
        out_shape=jax.ShapeDtypeStruct((bsz, rows, A_Q_W), BF16),
        scratch_shapes=[pltpu.VMEM((A_HEADS // A_KV_HEADS, A_HEAD_DIM, tq), F32),
                        pltpu.VMEM((A_Q_W // A_KV_HEADS, tq), F32)],
        compiler_params=_params(2),
        name="window_attention",
    )(sink, qt, *k_parts, *v_parts)


def _mla_proj_kernel(with_q, x_ref, mod_ref, g_ref, win_ref, wkr_ref, qg_ref, kvg_ref, wqt_ref, wkn_ref, wvt_ref,
                     cf_ref, sf_ref, ct_ref, s1_ref, s2_ref, *out_refs):
    h = _rms(x_ref[0], g_ref[...]) * (1.0 + mod_ref[0, 1:2, :]) + mod_ref[0, 0:1, :]
    hb = h.astype(BF16)
    lat = _dot(hb, win_ref[...])
    kvn = _rms(lat[:, C_Q_LORA:], kvg_ref[...]).astype(BF16)
    if with_q:
        qt_ref, k_ref, vt_ref = out_refs
        qn = _rms(lat[:, :C_Q_LORA], qg_ref[...]).astype(BF16)
        qt = _dot_nt(wqt_ref[...], qn)
        cf, sf = cf_ref[...], sf_ref[...]
        scale = LOG2E * (C_NOPE + C_ROPE) ** -0.5
        for hd in range(C_HEADS):
            base = C_SLAB * hd
            rope = _rope_rows(qt[base + C_NOPE:base + C_NOPE + C_ROPE], C_ROPE // 4, cf, sf)
            blk = jnp.concatenate([qt[base:base + C_NOPE], rope, qt[base + C_NOPE + C_ROPE:base + C_SLAB]], axis=0)
            qt_ref[0, base:base + C_SLAB, :] = (blk * scale).astype(BF16)
    else:
        k_ref, vt_ref = out_refs
    kn = _dot(kvn, wkn_ref[...])
    kr = _rope_lanes(_dot(hb, wkr_ref[...]), C_ROPE // 4, ct_ref[...], s1_ref[...], s2_ref[...])
    for hd in range(C_HEADS):
        k_ref[0, :, C_SLAB * hd:C_SLAB * (hd + 1)] = (kn[:, C_SLAB * hd:C_SLAB * (hd + 1)] + kr).astype(BF16)
    vt = _dot_nt(wvt_ref[...], kvn).astype(BF16)
    tk = vt_ref.shape[3]
    for j in range(vt.shape[1] // tk):
        _store_with_ones(vt_ref.at[0, j], vt[:, tk * j:tk * (j + 1)], C_HEADS, C_V)


def _mla_project(x, mod, g, weights, tabs, with_q, tm):
    bsz, rows, d = x.shape
    win, wkr, qg, kvg, wqt, wkn, wvt = weights
    cf, sf, ct, s1, s2 = tabs
    mod_map = (lambda b, i: (b, 0, 0)) if mod.shape[0] == bsz else (lambda b, i: (0, 0, 0))
    kw, vw, tk = C_HEADS * C_SLAB, C_HEADS * C_V_AUG, min(tm, LAT_KEY_TILE)
    out_specs = [pl.BlockSpec((1, tm, kw), lambda b, i: (b, i, 0)),
                 pl.BlockSpec((1, tm // tk, vw, tk), lambda b, i: (b, i, 0, 0))]
    out_shape = [jax.ShapeDtypeStruct((bsz, rows, kw), BF16),
                 jax.ShapeDtypeStruct((bsz, rows // tk, vw, tk), BF16)]
    if with_q:
        out_specs.insert(0, pl.BlockSpec((1, kw, tm), lambda b, i: (b, 0, i)))
        out_shape.insert(0, jax.ShapeDtypeStruct((bsz, kw, rows), BF16))
    return pl.pallas_call(
        functools.partial(_mla_proj_kernel, with_q),
        grid=(bsz, rows // tm),
        in_specs=[pl.BlockSpec((1, tm, d), lambda b, i: (b, i, 0)),
                  pl.BlockSpec((1, 6, d), mod_map),
                  _const_spec((1, d)), _const_spec(win.shape), _const_spec(wkr.shape),
                  _const_spec(qg.shape), _const_spec(kvg.shape),
                  _const_spec(wqt.shape), _const_spec(wkn.shape), _const_spec(wvt.shape),
                  pl.BlockSpec((16, tm), lambda b, i: (0, i)),
                  pl.BlockSpec((16, tm), lambda b, i: (0, i)),
                  pl.BlockSpec((tm, 128), lambda b, i: (i, 0)),
                  pl.BlockSpec((tm, 128), lambda b, i: (i, 0)),
                  pl.BlockSpec((tm, 128), lambda b, i: (i, 0))],
        out_specs=out_specs,
        out_shape=out_shape,
        compiler_params=_params(2),
        name="mla_project",
    )(x, mod, g, win, wkr, qg, kvg, wqt, wkn, wvt, cf, sf, ct, s1, s2)


C_HEADS_PER_STEP = 4


def _mla_attn_kernel(n_parts, qt_ref, *refs):
    k_parts, v_parts, (o_ref, acc_ref, slab_ref, s_ref) = _split_parts(refs, n_parts)
    chains = [(qt_ref[0, C_SLAB * j:C_SLAB * (j + 1), :], slice(C_SLAB * j, C_SLAB * (j + 1)),
               slice(C_V_AUG * j, C_V_AUG * (j + 1))) for j in range(C_HEADS_PER_STEP)]
    if n_parts == 2:
        outs = _attend_pipelined(chains, k_parts, v_parts, acc_ref, s_ref, C_V)
    else:
        outs = _attend(chains, k_parts, v_parts, acc_ref, C_V)
    for j in range(C_HEADS_PER_STEP):
        slab_ref[C_V * j:C_V * (j + 1), :] = outs[j]
    o_ref[0] = slab_ref[...].T.astype(BF16)


def _mla_attention(qt, k_parts, v_parts, tq):
    bsz, _, rows = qt.shape
    per = C_HEADS_PER_STEP
    in_specs = [pl.BlockSpec((1, per * C_SLAB, tq), lambda b, p, i: (b, p, i))]
    for k in k_parts:
        in_specs.append(pl.BlockSpec((1, k.shape[1], per * C_SLAB), lambda b, p, i: (b, 0, p)))
    for v in v_parts:
        in_specs.append(pl.BlockSpec((1, v.shape[1], per * C_V_AUG, v.shape[3]), lambda b, p, i: (b, 0, p, 0)))
    return pl.pallas_call(
        functools.partial(_mla_attn_kernel, len(k_parts)),
        grid=(bsz, C_HEADS // per, rows // tq),
        in_specs=in_specs,
        out_specs=pl.BlockSpec((1, tq, per * C_V), lambda b, p, i: (b, i, p)),
        out_shape=jax.ShapeDtypeStruct((bsz, rows, C_HEADS * C_V), BF16),
        scratch_shapes=[pltpu.VMEM((per, C_V_AUG, tq), F32), pltpu.VMEM((per * C_V, tq), F32),
                        pltpu.VMEM((PIPE_SLOTS, per, v_parts[0].shape[3], tq), F32)],
        compiler_params=_params(3),
        name="mla_attention",
    )(qt, *k_parts, *v_parts)


def _mix_ffn_kernel(n_o, *refs):
    o_refs = refs[:n_o]
    x_ref, mod_ref, g_ref, wout_ref, w13_ref, w2_ref, out_ref = refs[n_o:]
    y = None
    row = 0
    for o_ref in o_refs:
        width = o_ref.shape[2]
        part = _dot(o_ref[0], wout_ref[row:row + width, :])
        y = part if y is None else y + part
        row += width
    x1 = x_ref[0] + mod_ref[0, 2:3, :] * _rms(y, g_ref[1:2, :])
    h = _rms(x1, g_ref[2:3, :]) * (1.0 + mod_ref[0, 4:5, :]) + mod_ref[0, 3:4, :]
    hb = h.astype(BF16)
    f = None
    for c in range(FFN_HIDDEN // FFN_CHUNK):
        lo = FFN_CHUNK * c
        gate = _dot(hb, w13_ref[:, lo:lo + FFN_CHUNK])
        up = _dot(hb, w13_ref[:, FFN_HIDDEN + lo:FFN_HIDDEN + lo + FFN_CHUNK])
        act = (gate / (1.0 + jnp.exp(-gate)) * up).astype(BF16)
        part = _dot(act, w2_ref[lo:lo + FFN_CHUNK, :])
        f = part if f is None else f + part
    out_ref[0] = x1 + mod_ref[0, 5:6, :] * _rms(f, g_ref[3:4, :])


def _mix_ffn(o_parts, x, mod, g4, wout, w13, w2, tm):
    bsz, rows, d = x.shape
    mod_map = (lambda b, i: (b, 0, 0)) if mod.shape[0] == bsz else (lambda b, i: (0, 0, 0))
    in_specs = [pl.BlockSpec((1, tm, o.shape[2]), lambda b, i: (b, i, 0)) for o in o_parts]
    in_specs += [pl.BlockSpec((1, tm, d), lambda b, i: (b, i, 0)),
                 pl.BlockSpec((1, 6, d), mod_map),
                 _const_spec(g4.shape), _const_spec(wout.shape), _const_spec(w13.shape), _const_spec(w2.shape)]
    return pl.pallas_call(
        functools.partial(_mix_ffn_kernel, len(o_parts)),
        grid=(bsz, rows // tm),
        in_specs=in_specs,
        out_specs=pl.BlockSpec((1, tm, d), lambda b, i: (b, i, 0)),
        out_shape=jax.ShapeDtypeStruct((bsz, rows, d), F32),
        compiler_params=_params(2),
        name="mix_ffn",
    )(*o_parts, x, mod, g4, wout, w13, w2)


def _rope_tables(n, rot_dim):
    pos = jnp.arange(n, dtype=jnp.int32)
    row = (pos // GRID_W).astype(F32)
    col = (pos % GRID_W).astype(F32)
    axis_dim = rot_dim // 2
    inv_freq = ROPE_THETA ** (-jnp.arange(0, axis_dim, 2, dtype=F32) / axis_dim)
    ang = jnp.concatenate([row[:, None] * inv_freq, col[:, None] * inv_freq], axis=-1)
    return jnp.cos(ang), jnp.sin(ang)


def _lane_tables(cos, sin, lane_dims, quarter):
    dims = jnp.asarray(lane_dims, dtype=jnp.int32)
    live = dims >= 0
    safe = jnp.where(live, dims, 0)
    axis, within = safe // (2 * quarter), safe % (2 * quarter)
    idx = axis * quarter + within % quarter
    first = within < quarter
    c = jnp.where(live[None, :], cos[:, idx], 1.0)
    s = jnp.where(live[None, :], sin[:, idx], 0.0)
    return c, jnp.where(first[None, :], -s, 0.0), jnp.where(first[None, :], 0.0, s)


def _identity_tables(n, half):
    return (jnp.ones((half, n), F32), jnp.zeros((half, n), F32),
            jnp.ones((n, 128), F32), jnp.zeros((n, 128), F32), jnp.zeros((n, 128), F32))


def kernel(x, c, ctx, c_ctx, ada_w, ada_b, norm_g, ffn_w13, ffn_w2, ab_w_in, ab_w_out, ab_sink, diff_lambda,
           diff_subln_g, mla_w_in, mla_q_norm_g, mla_kv_norm_g, mla_wq_b, mla_wkv_b, mla_w_out):
    bsz, n, d = x.shape
    n_ctx = ctx.shape[1]
    depth = ada_w.shape[0]
    assert n_ctx == KEY_TILE and n % Q_TILE == 0 and n % ROW_TILE == 0 and bsz + 1 <= 16

    cond = jnp.zeros((16, d), F32).at[:bsz].set(c).at[bsz].set(c_ctx)
    mods = _ada_modulation(cond, ada_w, ada_b).reshape(depth, 16, 6, d)

    cos_ab, sin_ab = _rope_tables(n, A_HEAD_DIM)
    ab_lane = [l % 64 for l in range(128)]
    tabs_ab = (cos_ab.T, sin_ab.T) + _lane_tables(cos_ab, sin_ab, ab_lane, 16)
    cos_c, sin_c = _rope_tables(n, C_ROPE)
    c_lane = [l - C_NOPE if C_NOPE <= l < C_NOPE + C_ROPE else -1 for l in range(128)]
    tabs_c = (cos_c.T, sin_c.T) + _lane_tables(cos_c, sin_c, c_lane, C_ROPE // 4)

    for l in range(depth):
        last = l == depth - 1
        mod_x, mod_c = mods[l, :bsz], mods[l, bsz:bsz + 1]
        g_in = norm_g[l, 0:1]
        w13, w2 = ffn_w13[l].astype(BF16), ffn_w2[l].astype(BF16)
        if l % 2 == 0:
            e = l // 2
            w = ab_w_in[e]
            o1, o2 = AB_Q_W + A_KV_W, AB_Q_W + 2 * A_KV_W
            o3 = o2 + B_QK_W
            wqt = w[:, :AB_Q_W].T.astype(BF16)
            wk = jnp.concatenate([w[:, AB_Q_W:o1], w[:, o2:o3]], axis=1).astype(BF16)
            wvt = jnp.concatenate([w[:, o1:o2], w[:, o3:]], axis=1).T.astype(BF16)
            wout = ab_w_out[e].astype(BF16)
            lam_init = 0.8 - 0.6 * math.exp(-0.3 * l)
            qt_x, ka_x, kb_x, va_x, vb_x = _ab_project(x, mod_x, g_in, wqt, wk, wvt, tabs_ab, ROW_TILE)
            qt_c, ka_c, kb_c, va_c, vb_c = _ab_project(ctx, mod_c, g_in, wqt, wk, wvt,
                                                       _identity_tables(n_ctx, 32), n_ctx)
            lam_args = (diff_lambda[e], diff_subln_g[e], lam_init)
            oa = _window_attention(qt_x, [ka_x, ka_c], [va_x, va_c], ab_sink[e], True, Q_TILE)
            ob = _diff_attention(qt_x, [kb_x, kb_c], [vb_x, vb_c], *lam_args, Q_TILE)
            o_x = [oa, ob]
            if not last:
                oca = _window_attention(qt_c, [ka_c], [va_c], ab_sink[e], False, n_ctx)
                ocb = _diff_attention(qt_c, [kb_c], [vb_c], *lam_args, n_ctx)
                o_c = [oca, ocb]
        else:
            o = l // 2
            w = mla_w_in[o]
            win = w[:, :C_LAT_W].astype(BF16)
            wkr = jnp.zeros((d, C_SLAB), F32).at[:, C_NOPE:C_NOPE + C_ROPE].set(w[:, C_LAT_W:]).astype(BF16)
            wq = mla_wq_b[o].reshape(C_Q_LORA, C_HEADS, C_NOPE + C_ROPE)
            wq = jnp.pad(wq, ((0, 0), (0, 0), (0, C_SLAB - C_NOPE - C_ROPE)))
            wqt = wq.reshape(C_Q_LORA, C_HEADS * C_SLAB).T.astype(BF16)
            wkv = mla_wkv_b[o].reshape(C_KV_LORA, C_HEADS, C_NOPE + C_V)
            wkn = jnp.pad(wkv[:, :, :C_NOPE], ((0, 0), (0, 0), (0, C_SLAB - C_NOPE)))
            wkn = wkn.reshape(C_KV_LORA, C_HEADS * C_SLAB).astype(BF16)
            wvt = wkv[:, :, C_NOPE:].reshape(C_KV_LORA, C_HEADS * C_V).T.astype(BF16)
            weights = (win, wkr, mla_q_norm_g[o:o + 1], mla_kv_norm_g[o:o + 1], wqt, wkn, wvt)
            wout = mla_w_out[o].astype(BF16)
            qt_x, k_x, vt_x = _mla_project(x, mod_x, g_in, weights, tabs_c, True, ROW_TILE)
            ident = _identity_tables(n_ctx, 16)
            if last:
                k_c, vt_c = _mla_project(ctx, mod_c, g_in, weights, ident, False, n_ctx)
            else:
                qt_c, k_c, vt_c = _mla_project(ctx, mod_c, g_in, weights, ident, True, n_ctx)
            o_x = [_mla_attention(qt_x, [k_x, k_c], [vt_x, vt_c], Q_TILE)]
            if not last:
                o_c = [_mla_attention(qt_c, [k_c], [vt_c], n_ctx)]
        x = _mix_ffn(o_x, x, mod_x, norm_g[l], wout, w13, w2, ROW_TILE)
        if not last:
            ctx = _mix_ffn(o_c, ctx, mod_c, norm_g[l], wout, w13, w2, n_ctx)
    return x
```

```python
import functools
import math

import jax
import jax.numpy as jnp
from jax import lax
from jax.experimental import pallas as pl
from jax.experimental.pallas import tpu as pltpu

F32 = jnp.float32
BF16 = jnp.bfloat16

D_MODEL = 1024
GRID_W = 64
ROPE_THETA = 10000.0
NORM_EPS = 1e-6
NEG_INF = -1e30
WINDOW = 128
LOG2E = 1.4426950408889634

A_HEADS, A_KV_HEADS, A_HEAD_DIM = 8, 2, 64
B_HEADS, B_QK_DIM, B_V_DIM = 4, 64, 128
A_Q_W = A_HEADS * A_HEAD_DIM
A_KV_W = A_KV_HEADS * A_HEAD_DIM
B_QK_W = B_HEADS * 2 * B_QK_DIM
B_V_W = B_HEADS * B_V_DIM
AB_Q_W = A_Q_W + B_QK_W
AB_K_W = A_KV_W + B_QK_W
AB_V_W = A_KV_W + B_V_W

C_HEADS, C_Q_LORA, C_KV_LORA, C_NOPE, C_ROPE, C_V = 16, 384, 256, 64, 32, 64
C_SLAB = 128
C_LAT_W = C_Q_LORA + C_KV_LORA

FFN_HIDDEN = 2816
FFN_CHUNK = 256

ONES_ROWS = 16
A_V_AUG = A_HEAD_DIM + ONES_ROWS
B_V_AUG = B_V_DIM + ONES_ROWS
C_V_AUG = C_V + ONES_ROWS

KEY_TILE = 256
LAT_KEY_TILE = 512
PIPE_AHEAD = 2
PIPE_SLOTS = PIPE_AHEAD + 1
PIPE_GROUPS = 2
Q_TILE = 512
ROW_TILE = 512
VMEM_LIMIT = 56 * 1024 * 1024

_NT = (((1,), (1,)), ((), ()))


def _dot(a, b):
    return jnp.dot(a, b, preferred_element_type=F32)


def _dot_nt(a, b):
    return lax.dot_general(a, b, _NT, preferred_element_type=F32)


def _rms(x, g):
    return x * lax.rsqrt(jnp.mean(x * x, axis=-1, keepdims=True) + NORM_EPS) * g


def _const_spec(shape):
    zeros = (0,) * len(shape)
    return pl.BlockSpec(shape, lambda *_: zeros, pipeline_mode=pl.Buffered(1))


def _params(n_axes):
    return pltpu.CompilerParams(dimension_semantics=("arbitrary",) * n_axes,
                                vmem_limit_bytes=VMEM_LIMIT)


def _ada_kernel(c_ref, w_ref, b_ref, o_ref):
    c = c_ref[...]
    s = c / (1.0 + jnp.exp(-c))
    o_ref[0] = _dot(s.astype(BF16), w_ref[0].astype(BF16)) + b_ref[0]


def _ada_modulation(cond, ada_w, ada_b):
    depth, d, n = ada_w.shape
    tn = 1536
    return pl.pallas_call(
        _ada_kernel,
        grid=(depth, n // tn),
        in_specs=[pl.BlockSpec((16, d), lambda l, j: (0, 0)),
                  pl.BlockSpec((1, d, tn), lambda l, j: (l, 0, j)),
                  pl.BlockSpec((1, 1, tn), lambda l, j: (l, 0, j))],
        out_specs=pl.BlockSpec((1, 16, tn), lambda l, j: (l, 0, j)),
        out_shape=jax.ShapeDtypeStruct((depth, 16, n), F32),
        compiler_params=_params(2),
        name="ada_modulation",
    )(cond, ada_w, ada_b.reshape(depth, 1, n))


def _rope_rows(blk, quarter, cf, sf):
    a, b = blk[0:quarter], blk[quarter:2 * quarter]
    c, d = blk[2 * quarter:3 * quarter], blk[3 * quarter:4 * quarter]
    cr, sr = cf[0:quarter], sf[0:quarter]
    cc, sc = cf[quarter:2 * quarter], sf[quarter:2 * quarter]
    return jnp.concatenate([a * cr - b * sr, b * cr + a * sr, c * cc - d * sc, d * cc + c * sc], axis=0)


def _rope_lanes(x, quarter, ct, s1, s2):
    return x * ct + pltpu.roll(x, 128 - quarter, 1) * s1 + pltpu.roll(x, quarter, 1) * s2


def _store_with_ones(vt_ref, tile, heads, dv):
    row = lax.broadcasted_iota(jnp.int32, (ONES_ROWS, tile.shape[1]), 0)
    ones_blk = jnp.where(row == 0, 1.0, 0.0).astype(BF16)
    aug = dv + ONES_ROWS
    for h in range(heads):
        vt_ref[aug * h:aug * h + dv, :] = tile[dv * h:dv * (h + 1)]
        vt_ref[aug * h + dv:aug * (h + 1), :] = ones_blk


def _ab_proj_kernel(x_ref, mod_ref, g_ref, wqt_ref, wk_ref, wvt_ref, cf_ref, sf_ref, ct_ref, s1_ref, s2_ref,
                    qt_ref, ka_ref, kb_ref, vta_ref, vtb_ref):
    h = _rms(x_ref[0], g_ref[...]) * (1.0 + mod_ref[0, 1:2, :]) + mod_ref[0, 0:1, :]
    hb = h.astype(BF16)
    cf, sf = cf_ref[...], sf_ref[...]
    qt = _dot_nt(wqt_ref[...], hb)
    for g in range(AB_Q_W // 64):
        blk = _rope_rows(qt[64 * g:64 * g + 64], 16, cf, sf)
        qt_ref[0, 64 * g:64 * g + 64, :] = (blk * (LOG2E * 64 ** -0.5)).astype(BF16)
    kk = _dot(hb, wk_ref[...])
    ct, s1, s2 = ct_ref[...], s1_ref[...], s2_ref[...]
    ka_ref[0] = _rope_lanes(kk[:, :A_KV_W], 16, ct, s1, s2).astype(BF16)
    for c in range(B_QK_W // 128):
        lo = A_KV_W + 128 * c
        kb_ref[0, :, 128 * c:128 * c + 128] = _rope_lanes(kk[:, lo:lo + 128], 16, ct, s1, s2).astype(BF16)
    vt = _dot_nt(wvt_ref[...], hb).astype(BF16)
    tka, tkb = vta_ref.shape[3], vtb_ref.shape[3]
    for j in range(vt.shape[1] // tka):
        _store_with_ones(vta_ref.at[0, j], vt[:A_KV_W, tka * j:tka * (j + 1)], A_KV_HEADS, A_HEAD_DIM)
    for j in range(vt.shape[1] // tkb):
        _store_with_ones(vtb_ref.at[0, j], vt[A_KV_W:, tkb * j:tkb * (j + 1)], B_HEADS, B_V_DIM)


def _ab_project(x, mod, g, wqt, wk, wvt, tabs, tm):
    bsz, rows, d = x.shape
    cf, sf, ct, s1, s2 = tabs
    mod_map = (lambda b, i: (b, 0, 0)) if mod.shape[0] == bsz else (lambda b, i: (0, 0, 0))
    tkb = min(tm, LAT_KEY_TILE)
    return pl.pallas_call(
        _ab_proj_kernel,
        grid=(bsz, rows // tm),
        in_specs=[pl.BlockSpec((1, tm, d), lambda b, i: (b, i, 0)),
                  pl.BlockSpec((1, 6, d), mod_map),
                  _const_spec((1, d)), _const_spec(wqt.shape), _const_spec(wk.shape), _const_spec(wvt.shape),
                  pl.BlockSpec((32, tm), lambda b, i: (0, i)),
                  pl.BlockSpec((32, tm), lambda b, i: (0, i)),
                  pl.BlockSpec((tm, 128), lambda b, i: (i, 0)),
                  pl.BlockSpec((tm, 128), lambda b, i: (i, 0)),
                  pl.BlockSpec((tm, 128), lambda b, i: (i, 0))],
        out_specs=[pl.BlockSpec((1, AB_Q_W, tm), lambda b, i: (b, 0, i)),
                   pl.BlockSpec((1, tm, A_KV_W), lambda b, i: (b, i, 0)),
                   pl.BlockSpec((1, tm, B_QK_W), lambda b, i: (b, i, 0)),
                   pl.BlockSpec((1, tm // KEY_TILE, A_KV_HEADS * A_V_AUG, KEY_TILE), lambda b, i: (b, i, 0, 0)),
                   pl.BlockSpec((1, tm // tkb, B_HEADS * B_V_AUG, tkb), lambda b, i: (b, i, 0, 0))],
        out_shape=[jax.ShapeDtypeStruct((bsz, AB_Q_W, rows), BF16),
                   jax.ShapeDtypeStruct((bsz, rows, A_KV_W), BF16),
                   jax.ShapeDtypeStruct((bsz, rows, B_QK_W), BF16),
                   jax.ShapeDtypeStruct((bsz, rows // KEY_TILE, A_KV_HEADS * A_V_AUG, KEY_TILE), BF16),
                   jax.ShapeDtypeStruct((bsz, rows // tkb, B_HEADS * B_V_AUG, tkb), BF16)],
        compiler_params=_params(2),
        name="ab_project",
    )(x, mod, g, wqt, wk, wvt, cf, sf, ct, s1, s2)


def _tile_start(t, size):
    return t * size if isinstance(t, int) else pl.multiple_of(t * size, size)


def _normalise(acc, dv, l=None):
    return acc[:dv] * (1.0 / (acc[dv:dv + 1] if l is None else l))


def _attend(chains, k_parts, v_parts, acc_ref, dv, m0=None, window=None):
    n = len(chains)
    tq = chains[0][0].shape[1]
    ones_row = acc_ref.shape[1] > dv
    if ones_row and m0 is not None:
        row = lax.broadcasted_iota(jnp.int32, acc_ref.shape, 1)
        acc_ref[...] = jnp.where(row == dv, 1.0, 0.0)
    else:
        acc_ref[...] = jnp.zeros(acc_ref.shape, F32)
    carry = []
    for c in range(n):
        carry.append(jnp.full((1, tq), NEG_INF, F32) if m0 is None else m0[c])
        if not ones_row:
            carry.append(jnp.zeros((1, tq), F32) if m0 is None else jnp.ones((1, tq), F32))
    per = len(carry) // n

    def step(t, carry, k_ref, v_ref, valid_fn):
        tk = v_ref.shape[3]
        start = _tile_start(t, tk)
        valid = None if valid_fn is None else valid_fn(t)
        logits = [_dot(k_ref[0, pl.ds(start, tk), lanes], q_pad) for q_pad, lanes, _ in chains]
        out, updates = [], []
        for c in range(n):
            s = logits[c] if valid is None else jnp.where(valid, logits[c], NEG_INF)
            m = carry[per * c]
            m_new = jnp.maximum(m, jnp.max(s, axis=0, keepdims=True))
            p = jnp.exp2(s - m_new)
            alpha = jnp.exp2(m - m_new)
            out.append(m_new)
            if not ones_row:
                out.append(alpha * carry[per * c + 1] + jnp.sum(p, axis=0, keepdims=True))
            updates.append((alpha, p.astype(BF16)))
        for c, (alpha, p) in enumerate(updates):
            acc_ref[c] = alpha * acc_ref[c] + _dot(v_ref[0, t, chains[c][2], :], p)
        return tuple(out)

    carry = tuple(carry)
    for part, (k_ref, v_ref) in enumerate(zip(k_parts, v_parts)):
        n_tiles = v_ref.shape[1]
        if part == 0 and window is not None:
            lo, hi, valid_fn = window
            carry = lax.fori_loop(lo, hi, functools.partial(step, k_ref=k_ref, v_ref=v_ref, valid_fn=valid_fn), carry)
        elif n_tiles == 1:
            carry = step(0, carry, k_ref, v_ref, None)
        else:
            carry = lax.fori_loop(0, n_tiles, functools.partial(step, k_ref=k_ref, v_ref=v_ref, valid_fn=None), carry)
    return [_normalise(acc_ref[c], dv, None if ones_row else carry[per * c + 1]) for c in range(n)]


def _attend_pipelined(chains, k_parts, v_parts, acc_ref, s_ref, dv, m0=None, window=None):
    n = len(chains)
    tq = chains[0][0].shape[1]
    (k_lat, k_ctx), (v_lat, v_ctx) = k_parts, v_parts
    n_lat = v_lat.shape[1] if window is None else window[1]
    assert v_ctx.shape[1] == 1 and n_lat >= PIPE_AHEAD and acc_ref.shape[1] > dv and s_ref.shape[0] == PIPE_SLOTS
    if m0 is None:
        acc_ref[...] = jnp.zeros(acc_ref.shape, F32)
        ms = [jnp.full((1, tq), NEG_INF, F32) for _ in range(n)]
    else:
        row = lax.broadcasted_iota(jnp.int32, acc_ref.shape, 1)
        acc_ref[...] = jnp.where(row == dv, 1.0, 0.0)
        ms = list(m0)
    masks = {}

    def locate(t):
        if isinstance(t, int) and t == n_lat:
            return k_ctx, v_ctx, 0, None
        if window is None:
            return k_lat, v_lat, t, None
        if t not in masks:
            masks[t] = (jnp.clip(window[0] + t, 0, v_lat.shape[1] - 1), window[2](t))
        return (k_lat, v_lat) + masks[t]

    def park(t, slot, c):
        k_ref, v_ref, idx, valid = locate(t)
        tk = v_ref.shape[3]
        q_pad, lanes, _ = chains[c]
        s = _dot(k_ref[0, pl.ds(_tile_start(idx, tk), tk), lanes], q_pad)
        if valid is not None:
            s = jnp.where(valid, s, NEG_INF)
        s_ref[slot, c, 0:tk, :] = s
        return jnp.max(s, axis=0, keepdims=True)

    def finish(t, slot, c, m, tile_max):
        _, v_ref, idx, _ = locate(t)
        tk = v_ref.shape[3]
        m_new = jnp.maximum(m, tile_max)
        p = jnp.exp2(s_ref[slot, c, 0:tk, :] - m_new)
        acc_ref[c] = jnp.exp2(m - m_new) * acc_ref[c] + _dot(v_ref[0, idx, chains[c][2], :], p.astype(BF16))
        return m_new

    def sweep(t, slot, ms, maxima, t_ahead):
        ahead_slot = (slot + PIPE_AHEAD) % PIPE_SLOTS
        for c in range(n):
            if t_ahead is not None:
                maxima[ahead_slot][c] = park(t_ahead, ahead_slot, c)
            ms[c] = finish(t, slot, c, ms[c], maxima[slot][c])

    def unpack(carry):
        return list(carry[:n]), [list(carry[n * (1 + s):n * (2 + s)]) for s in range(PIPE_SLOTS)]

    per_iter = PIPE_SLOTS * PIPE_GROUPS

    def group(j, carry):
        ms, maxima = unpack(carry)
        for u in range(per_iter):
            sweep(per_iter * j + u, u % PIPE_SLOTS, ms, maxima, per_iter * j + u + PIPE_AHEAD)
        return tuple(ms + sum(maxima, []))

    maxima = [[jnp.full((1, tq), NEG_INF, F32) for _ in range(n)] for _ in range(PIPE_SLOTS)]
    for t in range(PIPE_AHEAD):
        maxima[t] = [park(t, t, c) for c in range(n)]
    n_iters = 0 if window is not None else (n_lat - PIPE_AHEAD) // per_iter
    if n_iters:
        ms, maxima = unpack(lax.fori_loop(0, n_iters, group, tuple(ms + sum(maxima, []))))
    for t in range(per_iter * n_iters, n_lat + 1):
        sweep(t, t % PIPE_SLOTS, ms, maxima, t + PIPE_AHEAD if t + PIPE_AHEAD <= n_lat else None)
    return [_normalise(acc_ref[c], dv) for c in range(n)]


def _split_parts(refs, n_parts):
    return refs[:n_parts], refs[n_parts:2 * n_parts], refs[2 * n_parts:]


B_HEADS_PER_STEP = 2


def _diff_attn_kernel(lam_init, n_parts, qt_ref, *refs):
    k_parts, v_parts, (lam_ref, g_ref, o_ref, acc_ref, s_ref) = _split_parts(refs, n_parts)
    chains = []
    for j in range(B_HEADS_PER_STEP):
        q = qt_ref[0, 128 * j:128 * (j + 1), :]
        row = lax.broadcasted_iota(jnp.int32, q.shape, 0)
        zero = jnp.zeros_like(q)
        slab, vrows = slice(128 * j, 128 * (j + 1)), slice(B_V_AUG * j, B_V_AUG * (j + 1))
        chains.append((jnp.where(row < B_QK_DIM, q, zero), slab, vrows))
        chains.append((jnp.where(row >= B_QK_DIM, q, zero), slab, vrows))
    if n_parts == 2:
        outs = _attend_pipelined(chains, k_parts, v_parts, acc_ref, s_ref, B_V_DIM)
    else:
        outs = _attend(chains, k_parts, v_parts, acc_ref, B_V_DIM)
    lv = lam_ref[...]
    lam = (jnp.exp(jnp.sum(lv[0:1] * lv[1:2], axis=1, keepdims=True))
           - jnp.exp(jnp.sum(lv[2:3] * lv[3:4], axis=1, keepdims=True)) + lam_init)
    for j in range(B_HEADS_PER_STEP):
        dlt = outs[2 * j] - lam * outs[2 * j + 1]
        y = dlt * lax.rsqrt(jnp.mean(dlt * dlt, axis=0, keepdims=True) + NORM_EPS) * g_ref[...] * (1.0 - lam_init)
        o_ref[0, :, B_V_DIM * j:B_V_DIM * (j + 1)] = y.T.astype(BF16)


def _diff_attention(qt, k_parts, v_parts, lam_vec, subln_g, lam_init, tq):
    bsz, _, rows = qt.shape
    per = B_HEADS_PER_STEP
    q_off = A_Q_W // (128 * per)
    in_specs = [pl.BlockSpec((1, 128 * per, tq), lambda b, h, i: (b, q_off + h, i))]
    for k in k_parts:
        in_specs.append(pl.BlockSpec((1, k.shape[1], 128 * per), lambda b, h, i: (b, 0, h)))
    for v in v_parts:
        in_specs.append(pl.BlockSpec((1, v.shape[1], B_V_AUG * per, v.shape[3]), lambda b, h, i: (b, 0, h, 0)))
    in_specs += [pl.BlockSpec(lam_vec.shape, lambda b, h, i: (0, 0)),
                 pl.BlockSpec((B_V_DIM, 1), lambda b, h, i: (0, 0))]
    return pl.pallas_call(
        functools.partial(_diff_attn_kernel, lam_init, len(k_parts)),
        grid=(bsz, B_HEADS // per, rows // tq),
        in_specs=in_specs,
        out_specs=pl.BlockSpec((1, tq, B_V_DIM * per), lambda b, h, i: (b, i, h)),
        out_shape=jax.ShapeDtypeStruct((bsz, rows, B_V_W), BF16),
        scratch_shapes=[pltpu.VMEM((2 * per, B_V_AUG, tq), F32),
                        pltpu.VMEM((PIPE_SLOTS, 2 * per, v_parts[0].shape[3], tq), F32)],
        compiler_params=_params(3),
        name="diff_attention",
    )(qt, *k_parts, *v_parts, lam_vec, subln_g.reshape(B_V_DIM, 1))


def _window_attn_kernel(windowed, n_parts, sink_ref, qt_ref, *refs):
    k_parts, v_parts, (o_ref, acc_ref, slab_ref, s_ref) = _split_parts(refs, n_parts)
    tq = qt_ref.shape[2]
    i = pl.program_id(1)
    group = A_HEADS // A_KV_HEADS
    zeros = jnp.zeros((A_HEAD_DIM, tq), BF16)
    window = None
    if windowed:
        per_q = tq // KEY_TILE
        n_keys = v_parts[0].shape[1] * KEY_TILE
        first = i * per_q - 1
        rel = (lax.broadcasted_iota(jnp.int32, (KEY_TILE, tq), 1)
               - lax.broadcasted_iota(jnp.int32, (KEY_TILE, tq), 0))

        def valid_fn(u):
            start = (first + u) * KEY_TILE
            reach = jnp.where((start >= 0) & (start < n_keys), WINDOW, -1)
            return jnp.abs(rel + (i * tq - start)) <= reach

        window = (first, per_q + 2, valid_fn)
    for kv in range(A_KV_HEADS):
        rows = slice(A_V_AUG * kv, A_V_AUG * (kv + 1))
        chains, m0 = [], []
        for hq in range(group * kv, group * (kv + 1)):
            qh = qt_ref[0, A_HEAD_DIM * hq:A_HEAD_DIM * (hq + 1), :]
            chains.append((jnp.concatenate([qh, zeros] if kv == 0 else [zeros, qh], axis=0), slice(None), rows))
            m0.append(jnp.full((1, tq), sink_ref[hq] * LOG2E, F32))
        if windowed:
            outs = _attend_pipelined(chains, k_parts, v_parts, acc_ref.at[kv], s_ref.at[kv], A_HEAD_DIM,
                                     m0=m0, window=window)
        else:
            outs = _attend(chains, k_parts, v_parts, acc_ref.at[kv], A_HEAD_DIM, m0=m0)
        for c in range(group):
            slab_ref[kv, A_HEAD_DIM * c:A_HEAD_DIM * (c + 1), :] = outs[c]
        width = group * A_HEAD_DIM
        o_ref[0, :, width * kv:width * (kv + 1)] = slab_ref[kv].T.astype(BF16)


def _window_attention(qt, k_parts, v_parts, sink, windowed, tq):
    bsz, _, rows = qt.shape
    in_specs = [pl.BlockSpec(memory_space=pltpu.SMEM),
                pl.BlockSpec((1, A_Q_W, tq), lambda b, i: (b, 0, i))]
    for k in k_parts:
        in_specs.append(pl.BlockSpec((1, k.shape[1], 128), lambda b, i: (b, 0, 0)))
    for v in v_parts:
        in_specs.append(pl.BlockSpec((1, v.shape[1], A_KV_HEADS * A_V_AUG, KEY_TILE), lambda b, i: (b, 0, 0, 0)))
    group = A_HEADS // A_KV_HEADS
    return pl.pallas_call(
        functools.partial(_window_attn_kernel, windowed, len(k_parts)),
        grid=(bsz, rows // tq),
        in_specs=in_specs,
        out_specs=pl.BlockSpec((1, tq, A_Q_W), lambda b, i: (b, i, 0)),
        out_shape=jax.ShapeDtypeStruct((bsz, rows, A_Q_W), BF16),
        scratch_shapes=[pltpu.VMEM((A_KV_HEADS, group, A_V_AUG, tq), F32),
                        pltpu.VMEM((A_KV_HEADS, group * A_HEAD_DIM, tq), F32),
                        pltpu.VMEM((A_KV_HEADS, PIPE_SLOTS, group, KEY_TILE, tq), F32)],
        compiler_params=_params(2),
        name="window_attention",
    )(sink, qt, *k_parts, *v_parts)


def _mla_proj_kernel(with_q, x_ref, mod_ref, g_ref, win_ref, wkr_ref, qg_ref, kvg_ref, wqt_ref, wkn_ref, wvt_ref,
                     cf_ref, sf_ref, ct_ref, s1_ref, s2_ref, *out_refs):
    h = _rms(x_ref[0], g_ref[...]) * (1.0 + mod_ref[0, 1:2, :]) + mod_ref[0, 0:1, :]
    hb = h.astype(BF16)
    lat = _dot(hb, win_ref[...])
    kvn = _rms(lat[:, C_Q_LORA:], kvg_ref[...]).astype(BF16)
    if with_q:
        qt_ref, k_ref, vt_ref = out_refs
        qn = _rms(lat[:, :C_Q_LORA], qg_ref[...]).astype(BF16)
        qt = _dot_nt(wqt_ref[...], qn)
        cf, sf = cf_ref[...], sf_ref[...]
        scale = LOG2E * (C_NOPE + C_ROPE) ** -0.5
        for hd in range(C_HEADS):
            base = C_SLAB * hd
            rope = _rope_rows(qt[base + C_NOPE:base + C_NOPE + C_ROPE], C_ROPE // 4, cf, sf)
            blk = jnp.concatenate([qt[base:base + C_NOPE], rope, qt[base + C_NOPE + C_ROPE:base + C_SLAB]], axis=0)
            qt_ref[0, base:base + C_SLAB, :] = (blk * scale).astype(BF16)
    else:
        k_ref, vt_ref = out_refs
    kn = _dot(kvn, wkn_ref[...])
    kr = _rope_lanes(_dot(hb, wkr_ref[...]), C_ROPE // 4, ct_ref[...], s1_ref[...], s2_ref[...])
    for hd in range(C_HEADS):
        k_ref[0, :, C_SLAB * hd:C_SLAB * (hd + 1)] = (kn[:, C_SLAB * hd:C_SLAB * (hd + 1)] + kr).astype(BF16)
    vt = _dot_nt(wvt_ref[...], kvn).astype(BF16)
    tk = vt_ref.shape[3]
    for j in range(vt.shape[1] // tk):
        _store_with_ones(vt_ref.at[0, j], vt[:, tk * j:tk * (j + 1)], C_HEADS, C_V)


def _mla_project(x, mod, g, weights, tabs, with_q, tm):
    bsz, rows, d = x.shape
    win, wkr, qg, kvg, wqt, wkn, wvt = weights
    cf, sf, ct, s1, s2 = tabs
    mod_map = (lambda b, i: (b, 0, 0)) if mod.shape[0] == bsz else (lambda b, i: (0, 0, 0))
    kw, vw, tk = C_HEADS * C_SLAB, C_HEADS * C_V_AUG, min(tm, LAT_KEY_TILE)
    out_specs = [pl.BlockSpec((1, tm, kw), lambda b, i: (b, i, 0)),
                 pl.BlockSpec((1, tm // tk, vw, tk), lambda b, i: (b, i, 0, 0))]
    out_shape = [jax.ShapeDtypeStruct((bsz, rows, kw), BF16),
                 jax.ShapeDtypeStruct((bsz, rows // tk, vw, tk), BF16)]
    if with_q:
        out_specs.insert(0, pl.BlockSpec((1, kw, tm), lambda b, i: (b, 0, i)))
        out_shape.insert(0, jax.ShapeDtypeStruct((bsz, kw, rows), BF16))
    return pl.pallas_call(
        functools.partial(_mla_proj_kernel, with_q),
        grid=(bsz, rows // tm),
        in_specs=[pl.BlockSpec((1, tm, d), lambda b, i: (b, i, 0)),
                  pl.BlockSpec((1, 6, d), mod_map),
                  _const_spec((1, d)), _const_spec(win.shape), _const_spec(wkr.shape),
                  _const_spec(qg.shape), _const_spec(kvg.shape),
                  _const_spec(wqt.shape), _const_spec(wkn.shape), _const_spec(wvt.shape),
                  pl.BlockSpec((16, tm), lambda b, i: (0, i)),
                  pl.BlockSpec((16, tm), lambda b, i: (0, i)),
                  pl.BlockSpec((tm, 128), lambda b, i: (i, 0)),
                  pl.BlockSpec((tm, 128), lambda b, i: (i, 0)),
                  pl.BlockSpec((tm, 128), lambda b, i: (i, 0))],
        out_specs=out_specs,
        out_shape=out_shape,
        compiler_params=_params(2),
        name="mla_project",
    )(x, mod, g, win, wkr, qg, kvg, wqt, wkn, wvt, cf, sf, ct, s1, s2)


C_HEADS_PER_STEP = 4


def _mla_attn_kernel(n_parts, qt_ref, *refs):
    k_parts, v_parts, (o_ref, acc_ref, slab_ref, s_ref) = _split_parts(refs, n_parts)
    chains = [(qt_ref[0, C_SLAB * j:C_SLAB * (j + 1), :], slice(C_SLAB * j, C_SLAB * (j + 1)),
               slice(C_V_AUG * j, C_V_AUG * (j + 1))) for j in range(C_HEADS_PER_STEP)]
    if n_parts == 2:
        outs = _attend_pipelined(chains, k_parts, v_parts, acc_ref, s_ref, C_V)
    else:
        outs = _attend(chains, k_parts, v_parts, acc_ref, C_V)
    for j in range(C_HEADS_PER_STEP):
        slab_ref[C_V * j:C_V * (j + 1), :] = outs[j]
    o_ref[0] = slab_ref[...].T.astype(BF16)


def _mla_attention(qt, k_parts, v_parts, tq):
    bsz, _, rows = qt.shape
    per = C_HEADS_PER_STEP
    in_specs = [pl.BlockSpec((1, per * C_SLAB, tq), lambda b, p, i: (b, p, i))]
    for k in k_parts:
        in_specs.append(pl.BlockSpec((1, k.shape[1], per * C_SLAB), lambda b, p, i: (b, 0, p)))
    for v in v_parts:
        in_specs.append(pl.BlockSpec((1, v.shape[1], per * C_V_AUG, v.shape[3]), lambda b, p, i: (b, 0, p, 0)))
    return pl.pallas_call(
        functools.partial(_mla_attn_kernel, len(k_parts)),
        grid=(bsz, C_HEADS // per, rows // tq),
        in_specs=in_specs,
        out_specs=pl.BlockSpec((1, tq, per * C_V), lambda b, p, i: (b, i, p)),
        out_shape=jax.ShapeDtypeStruct((bsz, rows, C_HEADS * C_V), BF16),
        scratch_shapes=[pltpu.VMEM((per, C_V_AUG, tq), F32), pltpu.VMEM((per * C_V, tq), F32),
                        pltpu.VMEM((PIPE_SLOTS, per, v_parts[0].shape[3], tq), F32)],
        compiler_params=_params(3),
        name="mla_attention",
    )(qt, *k_parts, *v_parts)


def _mix_ffn_kernel(n_o, *refs):
    o_refs = refs[:n_o]
    x_ref, mod_ref, g_ref, wout_ref, w13_ref, w2_ref, out_ref = refs[n_o:]
    y = None
    row = 0
    for o_ref in o_refs:
        width = o_ref.shape[2]
        part = _dot(o_ref[0], wout_ref[row:row + width, :])
        y = part if y is None else y + part
        row += width
    x1 = x_ref[0] + mod_ref[0, 2:3, :] * _rms(y, g_ref[1:2, :])
    h = _rms(x1, g_ref[2:3, :]) * (1.0 + mod_ref[0, 4:5, :]) + mod_ref[0, 3:4, :]
    hb = h.astype(BF16)
    f = None
    for c in range(FFN_HIDDEN // FFN_CHUNK):
        lo = FFN_CHUNK * c
        gate = _dot(hb, w13_ref[:, lo:lo + FFN_CHUNK])
        up = _dot(hb, w13_ref[:, FFN_HIDDEN + lo:FFN_HIDDEN + lo + FFN_CHUNK])
        act = (gate / (1.0 + jnp.exp(-gate)) * up).astype(BF16)
        part = _dot(act, w2_ref[lo:lo + FFN_CHUNK, :])
        f = part if f is None else f + part
    out_ref[0] = x1 + mod_ref[0, 5:6, :] * _rms(f, g_ref[3:4, :])


def _mix_ffn(o_parts, x, mod, g4, wout, w13, w2, tm):
    bsz, rows, d = x.shape
    mod_map = (lambda b, i: (b, 0, 0)) if mod.shape[0] == bsz else (lambda b, i: (0, 0, 0))
    in_specs = [pl.BlockSpec((1, tm, o.shape[2]), lambda b, i: (b, i, 0)) for o in o_parts]
    in_specs += [pl.BlockSpec((1, tm, d), lambda b, i: (b, i, 0)),
                 pl.BlockSpec((1, 6, d), mod_map),
                 _const_spec(g4.shape), _const_spec(wout.shape), _const_spec(w13.shape), _const_spec(w2.shape)]
    return pl.pallas_call(
        functools.partial(_mix_ffn_kernel, len(o_parts)),
        grid=(bsz, rows // tm),
        in_specs=in_specs,
        out_specs=pl.BlockSpec((1, tm, d), lambda b, i: (b, i, 0)),
        out_shape=jax.ShapeDtypeStruct((bsz, rows, d), F32),
        compiler_params=_params(2),
        name="mix_ffn",
    )(*o_parts, x, mod, g4, wout, w13, w2)


def _rope_tables(n, rot_dim):
    pos = jnp.arange(n, dtype=jnp.int32)
    row = (pos // GRID_W).astype(F32)
    col = (pos % GRID_W).astype(F32)
    axis_dim = rot_dim // 2
    inv_freq = ROPE_THETA ** (-jnp.arange(0, axis_dim, 2, dtype=F32) / axis_dim)
    ang = jnp.concatenate([row[:, None] * inv_freq, col[:, None] * inv_freq], axis=-1)
    return jnp.cos(ang), jnp.sin(ang)


def _lane_tables(cos, sin, lane_dims, quarter):
    dims = jnp.asarray(lane_dims, dtype=jnp.int32)
    live = dims >= 0
    safe = jnp.where(live, dims, 0)
    axis, within = safe // (2 * quarter), safe % (2 * quarter)
    idx = axis * quarter + within % quarter
    first = within < quarter
    c = jnp.where(live[None, :], cos[:, idx], 1.0)
    s = jnp.where(live[None, :], sin[:, idx], 0.0)
    return c, jnp.where(first[None, :], -s, 0.0), jnp.where(first[None, :], 0.0, s)


def _identity_tables(n, half):
    return (jnp.ones((half, n), F32), jnp.zeros((half, n), F32),
            jnp.ones((n, 128), F32), jnp.zeros((n, 128), F32), jnp.zeros((n, 128), F32))


def kernel(x, c, ctx, c_ctx, ada_w, ada_b, norm_g, ffn_w13, ffn_w2, ab_w_in, ab_w_out, ab_sink, diff_lambda,
           diff_subln_g, mla_w_in, mla_q_norm_g, mla_kv_norm_g, mla_wq_b, mla_wkv_b, mla_w_out):
    bsz, n, d = x.shape
    n_ctx = ctx.shape[1]
    depth = ada_w.shape[0]
    assert n_ctx == KEY_TILE and n % Q_TILE == 0 and n % ROW_TILE == 0 and bsz + 1 <= 16

    cond = jnp.zeros((16, d), F32).at[:bsz].set(c).at[bsz].set(c_ctx)
    mods = _ada_modulation(cond, ada_w, ada_b).reshape(depth, 16, 6, d)

    cos_ab, sin_ab = _rope_tables(n, A_HEAD_DIM)
    ab_lane = [l % 64 for l in range(128)]
    tabs_ab = (cos_ab.T, sin_ab.T) + _lane_tables(cos_ab, sin_ab, ab_lane, 16)
    cos_c, sin_c = _rope_tables(n, C_ROPE)
    c_lane = [l - C_NOPE if C_NOPE <= l < C_NOPE + C_ROPE else -1 for l in range(128)]
    tabs_c = (cos_c.T, sin_c.T) + _lane_tables(cos_c, sin_c, c_lane, C_ROPE // 4)

    for l in range(depth):
        last = l == depth - 1
        mod_x, mod_c = mods[l, :bsz], mods[l, bsz:bsz + 1]
        g_in = norm_g[l, 0:1]
        w13, w2 = ffn_w13[l].astype(BF16), ffn_w2[l].astype(BF16)
        if l % 2 == 0:
            e = l // 2
            w = ab_w_in[e]
            o1, o2 = AB_Q_W + A_KV_W, AB_Q_W + 2 * A_KV_W
            o3 = o2 + B_QK_W
            wqt = w[:, :AB_Q_W].T.astype(BF16)
            wk = jnp.concatenate([w[:, AB_Q_W:o1], w[:, o2:o3]], axis=1).astype(BF16)
            wvt = jnp.concatenate([w[:, o1:o2], w[:, o3:]], axis=1).T.astype(BF16)
            wout = ab_w_out[e].astype(BF16)
            lam_init = 0.8 - 0.6 * math.exp(-0.3 * l)
            qt_x, ka_x, kb_x, va_x, vb_x = _ab_project(x, mod_x, g_in, wqt, wk, wvt, tabs_ab, ROW_TILE)
            qt_c, ka_c, kb_c, va_c, vb_c = _ab_project(ctx, mod_c, g_in, wqt, wk, wvt,
                                                       _identity_tables(n_ctx, 32), n_ctx)
            lam_args = (diff_lambda[e], diff_subln_g[e], lam_init)
            oa = _window_attention(qt_x, [ka_x, ka_c], [va_x, va_c], ab_sink[e], True, Q_TILE)
            ob = _diff_attention(qt_x, [kb_x, kb_c], [vb_x, vb_c], *lam_args, Q_TILE)
            o_x = [oa, ob]
            if not last:
                oca = _window_attention(qt_c, [ka_c], [va_c], ab_sink[e], False, n_ctx)
                ocb = _diff_attention(qt_c, [kb_c], [vb_c], *lam_args, n_ctx)
                o_c = [oca, ocb]
        else:
            o = l // 2
            w = mla_w_in[o]
            win = w[:, :C_LAT_W].astype(BF16)
            wkr = jnp.zeros((d, C_SLAB), F32).at[:, C_NOPE:C_NOPE + C_ROPE].set(w[:, C_LAT_W:]).astype(BF16)
            wq = mla_wq_b[o].reshape(C_Q_LORA, C_HEADS, C_NOPE + C_ROPE)
            wq = jnp.pad(wq, ((0, 0), (0, 0), (0, C_SLAB - C_NOPE - C_ROPE)))
            wqt = wq.reshape(C_Q_LORA, C_HEADS * C_SLAB).T.astype(BF16)
            wkv = mla_wkv_b[o].reshape(C_KV_LORA, C_HEADS, C_NOPE + C_V)
            wkn = jnp.pad(wkv[:, :, :C_NOPE], ((0, 0), (0, 0), (0, C_SLAB - C_NOPE)))
            wkn = wkn.reshape(C_KV_LORA, C_HEADS * C_SLAB).astype(BF16)
            wvt = wkv[:, :, C_NOPE:].reshape(C_KV_LORA, C_HEADS * C_V).T.astype(BF16)
            weights = (win, wkr, mla_q_norm_g[o:o + 1], mla_kv_norm_g[o:o + 1], wqt, wkn, wvt)
            wout = mla_w_out[o].astype(BF16)
            qt_x, k_x, vt_x = _mla_project(x, mod_x, g_in, weights, tabs_c, True, ROW_TILE)
            ident = _identity_tables(n_ctx, 16)
            if last:
                k_c, vt_c = _mla_project(ctx, mod_c, g_in, weights, ident, False, n_ctx)
            else:
                qt_c, k_c, vt_c = _mla_project(ctx, mod_c, g_in, weights, ident, True, n_ctx)
            o_x = [_mla_attention(qt_x, [k_x, k_c], [vt_x, vt_c], Q_TILE)]
            if not last:
                o_c = [_mla_attention(qt_c, [k_c], [vt_c], n_ctx)]
        x = _mix_ffn(o_x, x, mod_x, norm_g[l], wout, w13, w2, ROW_TILE)
        if not last:
            ctx = _mix_ffn(o_c, ctx, mod_c, norm_g[l], wout, w13, w2, n_ctx)
    return x
```

```python
import functools
import math
from typing import NamedTuple

import jax
import jax.numpy as jnp
from jax import lax
from jax.experimental import pallas as pl
from jax.experimental.pallas import tpu as pltpu

F32 = jnp.float32
BF16 = jnp.bfloat16

D_MODEL = 1024
GRID_W = 64
ROPE_THETA = 10000.0
NORM_EPS = 1e-6
NEG_INF = -1e30
WINDOW = 128
LOG2E = 1.4426950408889634

A_HEADS, A_KV_HEADS, A_HEAD_DIM = 8, 2, 64
B_HEADS, B_QK_DIM, B_V_DIM = 4, 64, 128
A_Q_W = A_HEADS * A_HEAD_DIM
A_KV_W = A_KV_HEADS * A_HEAD_DIM
B_QK_W = B_HEADS * 2 * B_QK_DIM
B_V_W = B_HEADS * B_V_DIM
AB_Q_W = A_Q_W + B_QK_W
AB_K_W = A_KV_W + B_QK_W
AB_V_W = A_KV_W + B_V_W

C_HEADS, C_Q_LORA, C_KV_LORA, C_NOPE, C_ROPE, C_V = 16, 384, 256, 64, 32, 64
C_SLAB = 128
C_LAT_W = C_Q_LORA + C_KV_LORA

FFN_HIDDEN = 2816
FFN_CHUNK = 256

ONES_ROWS = 16
A_V_AUG = A_HEAD_DIM + ONES_ROWS
B_V_AUG = B_V_DIM + ONES_ROWS
C_V_AUG = C_V + ONES_ROWS

KEY_TILE = 256
LAT_KEY_TILE = 512
PIPE_AHEAD = 2
PIPE_SLOTS = PIPE_AHEAD + 1
PIPE_GROUPS = 2
Q_TILE = 512
ROW_TILE = 512
VMEM_LIMIT = 56 * 1024 * 1024

_NT = (((1,), (1,)), ((), ()))


def _dot(a, b):
    return jnp.dot(a, b, preferred_element_type=F32)


def _dot_nt(a, b):
    return lax.dot_general(a, b, _NT, preferred_element_type=F32)


def _rms(x, g):
    return x * lax.rsqrt(jnp.mean(x * x, axis=-1, keepdims=True) + NORM_EPS) * g


def _const_spec(shape):
    zeros = (0,) * len(shape)
    return pl.BlockSpec(shape, lambda *_: zeros, pipeline_mode=pl.Buffered(1))


def _params(n_axes):
    return pltpu.CompilerParams(dimension_semantics=("arbitrary",) * n_axes,
                                vmem_limit_bytes=VMEM_LIMIT)


def _ada_kernel(c_ref, w_ref, b_ref, o_ref):
    c = c_ref[...]
    s = c / (1.0 + jnp.exp(-c))
    o_ref[0] = _dot(s.astype(BF16), w_ref[0].astype(BF16)) + b_ref[0]


def _ada_modulation(cond, ada_w, ada_b):
    depth, d, n = ada_w.shape
    tn = 1536
    return pl.pallas_call(
        _ada_kernel,
        grid=(depth, n // tn),
        in_specs=[pl.BlockSpec((16, d), lambda l, j: (0, 0)),
                  pl.BlockSpec((1, d, tn), lambda l, j: (l, 0, j)),
                  pl.BlockSpec((1, 1, tn), lambda l, j: (l, 0, j))],
        out_specs=pl.BlockSpec((1, 16, tn), lambda l, j: (l, 0, j)),
        out_shape=jax.ShapeDtypeStruct((depth, 16, n), F32),
        compiler_params=_params(2),
        name="ada_modulation",
    )(cond, ada_w, ada_b.reshape(depth, 1, n))


def _rope_rows(blk, quarter, cf, sf):
    a, b = blk[0:quarter], blk[quarter:2 * quarter]
    c, d = blk[2 * quarter:3 * quarter], blk[3 * quarter:4 * quarter]
    cr, sr = cf[0:quarter], sf[0:quarter]
    cc, sc = cf[quarter:2 * quarter], sf[quarter:2 * quarter]
    return jnp.concatenate([a * cr - b * sr, b * cr + a * sr, c * cc - d * sc, d * cc + c * sc], axis=0)


def _rope_lanes(x, quarter, ct, s1, s2):
    return x * ct + pltpu.roll(x, 128 - quarter, 1) * s1 + pltpu.roll(x, quarter, 1) * s2


def _store_with_ones(vt_ref, tile, heads, dv):
    row = lax.broadcasted_iota(jnp.int32, (ONES_ROWS, tile.shape[1]), 0)
    ones_blk = jnp.where(row == 0, 1.0, 0.0).astype(BF16)
    aug = dv + ONES_ROWS
    for h in range(heads):
        vt_ref[aug * h:aug * h + dv, :] = tile[dv * h:dv * (h + 1)]
        vt_ref[aug * h + dv:aug * (h + 1), :] = ones_blk


def _ab_proj_kernel(x_ref, mod_ref, g_ref, wqt_ref, wk_ref, wvt_ref, cf_ref, sf_ref, ct_ref, s1_ref, s2_ref,
                    qt_ref, ka_ref, kb_ref, vta_ref, vtb_ref):
    h = _rms(x_ref[0], g_ref[...]) * (1.0 + mod_ref[0, 1:2, :]) + mod_ref[0, 0:1, :]
    hb = h.astype(BF16)
    cf, sf = cf_ref[...], sf_ref[...]
    qt = _dot_nt(wqt_ref[...], hb)
    for g in range(AB_Q_W // 64):
        blk = _rope_rows(qt[64 * g:64 * g + 64], 16, cf, sf)
        qt_ref[0, 64 * g:64 * g + 64, :] = (blk * (LOG2E * 64 ** -0.5)).astype(BF16)
    kk = _dot(hb, wk_ref[...])
    ct, s1, s2 = ct_ref[...], s1_ref[...], s2_ref[...]
    ka_ref[0] = _rope_lanes(kk[:, :A_KV_W], 16, ct, s1, s2).astype(BF16)
    for c in range(B_QK_W // 128):
        lo = A_KV_W + 128 * c
        kb_ref[0, :, 128 * c:128 * c + 128] = _rope_lanes(kk[:, lo:lo + 128], 16, ct, s1, s2).astype(BF16)
    vt = _dot_nt(wvt_ref[...], hb).astype(BF16)
    tka, tkb = vta_ref.shape[3], vtb_ref.shape[3]
    for j in range(vt.shape[1] // tka):
        _store_with_ones(vta_ref.at[0, j], vt[:A_KV_W, tka * j:tka * (j + 1)], A_KV_HEADS, A_HEAD_DIM)
    for j in range(vt.shape[1] // tkb):
        _store_with_ones(vtb_ref.at[0, j], vt[A_KV_W:, tkb * j:tkb * (j + 1)], B_HEADS, B_V_DIM)


def _ab_project(x, mod, g, wqt, wk, wvt, tabs, tm):
    bsz, rows, d = x.shape
    cf, sf, ct, s1, s2 = tabs
    mod_map = (lambda b, i: (b, 0, 0)) if mod.shape[0] == bsz else (lambda b, i: (0, 0, 0))
    tkb = min(tm, LAT_KEY_TILE)
    return pl.pallas_call(
        _ab_proj_kernel,
        grid=(bsz, rows // tm),
        in_specs=[pl.BlockSpec((1, tm, d), lambda b, i: (b, i, 0)),
                  pl.BlockSpec((1, 6, d), mod_map),
                  _const_spec((1, d)), _const_spec(wqt.shape), _const_spec(wk.shape), _const_spec(wvt.shape),
                  pl.BlockSpec((32, tm), lambda b, i: (0, i)),
                  pl.BlockSpec((32, tm), lambda b, i: (0, i)),
                  pl.BlockSpec((tm, 128), lambda b, i: (i, 0)),
                  pl.BlockSpec((tm, 128), lambda b, i: (i, 0)),
                  pl.BlockSpec((tm, 128), lambda b, i: (i, 0))],
        out_specs=[pl.BlockSpec((1, AB_Q_W, tm), lambda b, i: (b, 0, i)),
                   pl.BlockSpec((1, tm, A_KV_W), lambda b, i: (b, i, 0)),
                   pl.BlockSpec((1, tm, B_QK_W), lambda b, i: (b, i, 0)),
                   pl.BlockSpec((1, tm // KEY_TILE, A_KV_HEADS * A_V_AUG, KEY_TILE), lambda b, i: (b, i, 0, 0)),
                   pl.BlockSpec((1, tm // tkb, B_HEADS * B_V_AUG, tkb), lambda b, i: (b, i, 0, 0))],
        out_shape=[jax.ShapeDtypeStruct((bsz, AB_Q_W, rows), BF16),
                   jax.ShapeDtypeStruct((bsz, rows, A_KV_W), BF16),
                   jax.ShapeDtypeStruct((bsz, rows, B_QK_W), BF16),
                   jax.ShapeDtypeStruct((bsz, rows // KEY_TILE, A_KV_HEADS * A_V_AUG, KEY_TILE), BF16),
                   jax.ShapeDtypeStruct((bsz, rows // tkb, B_HEADS * B_V_AUG, tkb), BF16)],
        compiler_params=_params(2),
        name="ab_project",
    )(x, mod, g, wqt, wk, wvt, cf, sf, ct, s1, s2)


def _tile_start(t, size):
    return t * size if isinstance(t, int) else pl.multiple_of(t * size, size)


def _normalise(acc, dv):
    return acc[:dv] * (1.0 / acc[dv:dv + 1])


class _KeyTile(NamedTuple):
    k_ref: object
    v_ref: object
    index: object
    offset: int
    size: int
    valid: object = None


def _key_sequence(k_parts, v_parts):
    if len(k_parts) == 1:
        assert v_parts[0].shape[1] == 1
        return [_KeyTile(k_parts[0], v_parts[0], 0, 0, v_parts[0].shape[3])], None
    (k_lat, k_ctx), (v_lat, v_ctx) = k_parts, v_parts
    n_lat, tk = v_lat.shape[1], v_lat.shape[3]
    assert v_ctx.shape[1] == 1 and n_lat >= 3 and tk % 256 == 0
    halves = lambda i: [_KeyTile(k_lat, v_lat, i, 0, tk // 2), _KeyTile(k_lat, v_lat, i, tk // 2, tk // 2)]
    tiles = (halves(0) + [_KeyTile(k_lat, v_lat, i, 0, tk) for i in range(1, n_lat - 1)] + halves(n_lat - 1)
             + [_KeyTile(k_ctx, v_ctx, 0, 0, v_ctx.shape[3])])
    return tiles, (2, n_lat)


def _attend_pipelined(chains, tiles, acc_ref, s_ref, dv, uniform=None, m0=None):
    n = len(chains)
    tq = chains[0][0].shape[1]
    n_pos = len(tiles)
    assert acc_ref.shape[1] > dv and s_ref.shape[0] == PIPE_SLOTS
    if m0 is None:
        acc_ref[...] = jnp.zeros(acc_ref.shape, F32)
        ms = [jnp.full((1, tq), NEG_INF, F32) for _ in range(n)]
    else:
        row = lax.broadcasted_iota(jnp.int32, acc_ref.shape, 1)
        acc_ref[...] = jnp.where(row == dv, 1.0, 0.0)
        ms = list(m0)

    def locate(p):
        if isinstance(p, int):
            return tiles[p]
        lo = uniform[0]
        return tiles[lo]._replace(index=p - (lo - tiles[lo].index))

    def park(p, slot, c):
        tile = locate(p)
        q_pad, lanes, _ = chains[c]
        start = _tile_start(tile.index, tile.v_ref.shape[3]) + tile.offset
        s = _dot(tile.k_ref[0, pl.ds(start, tile.size), lanes], q_pad)
        if tile.valid is not None:
            s = jnp.where(tile.valid, s, NEG_INF)
        s_ref[slot, c, 0:tile.size, :] = s
        return jnp.max(s, axis=0, keepdims=True)

    def finish(p, slot, c, m, tile_max):
        tile = locate(p)
        m_new = jnp.maximum(m, tile_max)
        prob = jnp.exp2(s_ref[slot, c, 0:tile.size, :] - m_new)
        v_tile = tile.v_ref[0, tile.index, chains[c][2], tile.offset:tile.offset + tile.size]
        acc_ref[c] = jnp.exp2(m - m_new) * acc_ref[c] + _dot(v_tile, prob.astype(BF16))
        return m_new

    def sweep(p, slot, ms, maxima, p_ahead):
        ahead_slot = (slot + PIPE_AHEAD) % PIPE_SLOTS
        for c in range(n):
            if p_ahead is not None:
                maxima[ahead_slot][c] = park(p_ahead, ahead_slot, c)
            ms[c] = finish(p, slot, c, ms[c], maxima[slot][c])

    def unpack(carry):
        return list(carry[:n]), [list(carry[n * (1 + s):n * (2 + s)]) for s in range(PIPE_SLOTS)]

    per_iter = PIPE_SLOTS * PIPE_GROUPS
    loop_lo, n_iters = 0, 0
    if uniform is not None:
        loop_lo, n_iters = uniform[0], (uniform[1] - uniform[0] - PIPE_AHEAD) // per_iter

    def group(j, carry):
        ms, maxima = unpack(carry)
        for u in range(per_iter):
            p = loop_lo + per_iter * j + u
            sweep(p, (loop_lo + u) % PIPE_SLOTS, ms, maxima, p + PIPE_AHEAD)
        return tuple(ms + sum(maxima, []))

    def static_sweeps(lo, hi, ms, maxima):
        for p in range(lo, hi):
            sweep(p, p % PIPE_SLOTS, ms, maxima, p + PIPE_AHEAD if p + PIPE_AHEAD < n_pos else None)

    maxima = [[jnp.full((1, tq), NEG_INF, F32) for _ in range(n)] for _ in range(PIPE_SLOTS)]
    for p in range(min(PIPE_AHEAD, n_pos)):
        maxima[p] = [park(p, p, c) for c in range(n)]
    static_sweeps(0, loop_lo, ms, maxima)
    if n_iters:
        ms, maxima = unpack(lax.fori_loop(0, n_iters, group, tuple(ms + sum(maxima, []))))
    static_sweeps(loop_lo + per_iter * n_iters, n_pos, ms, maxima)
    return [_normalise(acc_ref[c], dv) for c in range(n)]


def _split_parts(refs, n_parts):
    return refs[:n_parts], refs[n_parts:2 * n_parts], refs[2 * n_parts:]


B_HEADS_PER_STEP = 2


def _diff_attn_kernel(lam_init, n_parts, qt_ref, *refs):
    k_parts, v_parts, (lam_ref, g_ref, o_ref, acc_ref, s_ref) = _split_parts(refs, n_parts)
    chains = []
    for j in range(B_HEADS_PER_STEP):
        q = qt_ref[0, 128 * j:128 * (j + 1), :]
        row = lax.broadcasted_iota(jnp.int32, q.shape, 0)
        zero = jnp.zeros_like(q)
        slab, vrows = slice(128 * j, 128 * (j + 1)), slice(B_V_AUG * j, B_V_AUG * (j + 1))
        chains.append((jnp.where(row < B_QK_DIM, q, zero), slab, vrows))
        chains.append((jnp.where(row >= B_QK_DIM, q, zero), slab, vrows))
    tiles, uniform = _key_sequence(k_parts, v_parts)
    outs = _attend_pipelined(chains, tiles, acc_ref, s_ref, B_V_DIM, uniform=uniform)
    lv = lam_ref[...]
    lam = (jnp.exp(jnp.sum(lv[0:1] * lv[1:2], axis=1, keepdims=True))
           - jnp.exp(jnp.sum(lv[2:3] * lv[3:4], axis=1, keepdims=True)) + lam_init)
    for j in range(B_HEADS_PER_STEP):
        dlt = outs[2 * j] - lam * outs[2 * j + 1]
        y = dlt * lax.rsqrt(jnp.mean(dlt * dlt, axis=0, keepdims=True) + NORM_EPS) * g_ref[...] * (1.0 - lam_init)
        o_ref[0, :, B_V_DIM * j:B_V_DIM * (j + 1)] = y.T.astype(BF16)


def _diff_attention(qt, k_parts, v_parts, lam_vec, subln_g, lam_init, tq):
    bsz, _, rows = qt.shape
    per = B_HEADS_PER_STEP
    q_off = A_Q_W // (128 * per)
    in_specs = [pl.BlockSpec((1, 128 * per, tq), lambda b, h, i: (b, q_off + h, i))]
    for k in k_parts:
        in_specs.append(pl.BlockSpec((1, k.shape[1], 128 * per), lambda b, h, i: (b, 0, h)))
    for v in v_parts:
        in_specs.append(pl.BlockSpec((1, v.shape[1], B_V_AUG * per, v.shape[3]), lambda b, h, i: (b, 0, h, 0)))
    in_specs += [pl.BlockSpec(lam_vec.shape, lambda b, h, i: (0, 0)),
                 pl.BlockSpec((B_V_DIM, 1), lambda b, h, i: (0, 0))]
    return pl.pallas_call(
        functools.partial(_diff_attn_kernel, lam_init, len(k_parts)),
        grid=(bsz, B_HEADS // per, rows // tq),
        in_specs=in_specs,
        out_specs=pl.BlockSpec((1, tq, B_V_DIM * per), lambda b, h, i: (b, i, h)),
        out_shape=jax.ShapeDtypeStruct((bsz, rows, B_V_W), BF16),
        scratch_shapes=[pltpu.VMEM((2 * per, B_V_AUG, tq), F32),
                        pltpu.VMEM((PIPE_SLOTS, 2 * per, v_parts[0].shape[3], tq), F32)],
        compiler_params=_params(3),
        name="diff_attention",
    )(qt, *k_parts, *v_parts, lam_vec, subln_g.reshape(B_V_DIM, 1))


def _window_attn_kernel(windowed, n_parts, sink_ref, qt_ref, *refs):
    k_parts, v_parts, (o_ref, acc_ref, slab_ref, s_ref) = _split_parts(refs, n_parts)
    tq = qt_ref.shape[2]
    i = pl.program_id(1)
    group = A_HEADS // A_KV_HEADS
    zeros = jnp.zeros((A_HEAD_DIM, tq), BF16)
    if not windowed:
        tiles, _ = _key_sequence(k_parts, v_parts)
    else:
        assert 2 * WINDOW == KEY_TILE and tq % KEY_TILE == 0
        per_q, n_tiles = tq // KEY_TILE, v_parts[0].shape[1]
        first = i * per_q - 1
        rel = (lax.broadcasted_iota(jnp.int32, (KEY_TILE, tq), 1)
               - lax.broadcasted_iota(jnp.int32, (KEY_TILE, tq), 0))

        def window_tile(u, offset, size):
            t = first + u
            reach = jnp.where((t >= 0) & (t < n_tiles), WINDOW, -1)
            valid = jnp.abs(rel[:size] + (i * tq - t * KEY_TILE - offset)) <= reach
            return _KeyTile(k_parts[0], v_parts[0], jnp.clip(t, 0, n_tiles - 1), offset, size, valid)

        tiles = ([window_tile(0, WINDOW, WINDOW)] + [window_tile(u, 0, KEY_TILE) for u in range(1, per_q + 1)]
                 + [window_tile(per_q + 1, 0, WINDOW), _KeyTile(k_parts[1], v_parts[1], 0, 0, KEY_TILE)])
    for kv in range(A_KV_HEADS):
        rows = slice(A_V_AUG * kv, A_V_AUG * (kv + 1))
        chains, m0 = [], []
        for hq in range(group * kv, group * (kv + 1)):
            qh = qt_ref[0, A_HEAD_DIM * hq:A_HEAD_DIM * (hq + 1), :]
            chains.append((jnp.concatenate([qh, zeros] if kv == 0 else [zeros, qh], axis=0), slice(None), rows))
            m0.append(jnp.full((1, tq), sink_ref[hq] * LOG2E, F32))
        outs = _attend_pipelined(chains, tiles, acc_ref.at[kv], s_ref.at[kv], A_HEAD_DIM, m0=m0)
        for c in range(group):
            slab_ref[kv, A_HEAD_DIM * c:A_HEAD_DIM * (c + 1), :] = outs[c]
        width = group * A_HEAD_DIM
        o_ref[0, :, width * kv:width * (kv + 1)] = slab_ref[kv].T.astype(BF16)


def _window_attention(qt, k_parts, v_parts, sink, windowed, tq):
    bsz, _, rows = qt.shape
    in_specs = [pl.BlockSpec(memory_space=pltpu.SMEM),
                pl.BlockSpec((1, A_Q_W, tq), lambda b, i: (b, 0, i))]
    for k in k_parts:
        in_specs.append(pl.BlockSpec((1, k.shape[1], 128), lambda b, i: (b, 0, 0)))
    for v in v_parts:
        in_specs.append(pl.BlockSpec((1, v.shape[1], A_KV_HEADS * A_V_AUG, KEY_TILE), lambda b, i: (b, 0, 0, 0)))
    group = A_HEADS // A_KV_HEADS
    return pl.pallas_call(
        functools.partial(_window_attn_kernel, windowed, len(k_parts)),
        grid=(bsz, rows // tq),
        in_specs=in_specs,
        out_specs=pl.BlockSpec((1, tq, A_Q_W), lambda b, i: (b, i, 0)),
        out_shape=jax.ShapeDtypeStruct((bsz, rows, A_Q_W), BF16),
        scratch_shapes=[pltpu.VMEM((A_KV_HEADS, group, A_V_AUG, tq), F32),
                        pltpu.VMEM((A_KV_HEADS, group * A_HEAD_DIM, tq), F32),
                        pltpu.VMEM((A_KV_HEADS, PIPE_SLOTS, group, KEY_TILE, tq), F32)],
        compiler_params=_params(2),
        name="window_attention",
    )(sink, qt, *k_parts, *v_parts)


def _mla_proj_kernel(with_q, x_ref, mod_ref, g_ref, win_ref, wkr_ref, qg_ref, kvg_ref, wqt_ref, wkn_ref, wvt_ref,
                     cf_ref, sf_ref, ct_ref, s1_ref, s2_ref, *out_refs):
    h = _rms(x_ref[0], g_ref[...]) * (1.0 + mod_ref[0, 1:2, :]) + mod_ref[0, 0:1, :]
    hb = h.astype(BF16)
    lat = _dot(hb, win_ref[...])
    kvn = _rms(lat[:, C_Q_LORA:], kvg_ref[...]).astype(BF16)
    if with_q:
        qt_ref, k_ref, vt_ref = out_refs
        qn = _rms(lat[:, :C_Q_LORA], qg_ref[...]).astype(BF16)
        qt = _dot_nt(wqt_ref[...], qn)
        cf, sf = cf_ref[...], sf_ref[...]
        scale = LOG2E * (C_NOPE + C_ROPE) ** -0.5
        for hd in range(C_HEADS):
            base = C_SLAB * hd
            rope = _rope_rows(qt[base + C_NOPE:base + C_NOPE + C_ROPE], C_ROPE // 4, cf, sf)
            blk = jnp.concatenate([qt[base:base + C_NOPE], rope, qt[base + C_NOPE + C_ROPE:base + C_SLAB]], axis=0)
            qt_ref[0, base:base + C_SLAB, :] = (blk * scale).astype(BF16)
    else:
        k_ref, vt_ref = out_refs
    kn = _dot(kvn, wkn_ref[...])
    kr = _rope_lanes(_dot(hb, wkr_ref[...]), C_ROPE // 4, ct_ref[...], s1_ref[...], s2_ref[...])
    for hd in range(C_HEADS):
        k_ref[0, :, C_SLAB * hd:C_SLAB * (hd + 1)] = (kn[:, C_SLAB * hd:C_SLAB * (hd + 1)] + kr).astype(BF16)
    vt = _dot_nt(wvt_ref[...], kvn).astype(BF16)
    tk = vt_ref.shape[3]
    for j in range(vt.shape[1] // tk):
        _store_with_ones(vt_ref.at[0, j], vt[:, tk * j:tk * (j + 1)], C_HEADS, C_V)


def _mla_project(x, mod, g, weights, tabs, with_q, tm):
    bsz, rows, d = x.shape
    win, wkr, qg, kvg, wqt, wkn, wvt = weights
    cf, sf, ct, s1, s2 = tabs
    mod_map = (lambda b, i: (b, 0, 0)) if mod.shape[0] == bsz else (lambda b, i: (0, 0, 0))
    kw, vw, tk = C_HEADS * C_SLAB, C_HEADS * C_V_AUG, min(tm, LAT_KEY_TILE)
    out_specs = [pl.BlockSpec((1, tm, kw), lambda b, i: (b, i, 0)),
                 pl.BlockSpec((1, tm // tk, vw, tk), lambda b, i: (b, i, 0, 0))]
    out_shape = [jax.ShapeDtypeStruct((bsz, rows, kw), BF16),
                 jax.ShapeDtypeStruct((bsz, rows // tk, vw, tk), BF16)]
    if with_q:
        out_specs.insert(0, pl.BlockSpec((1, kw, tm), lambda b, i: (b, 0, i)))
        out_shape.insert(0, jax.ShapeDtypeStruct((bsz, kw, rows), BF16))
    return pl.pallas_call(
        functools.partial(_mla_proj_kernel, with_q),
        grid=(bsz, rows // tm),
        in_specs=[pl.BlockSpec((1, tm, d), lambda b, i: (b, i, 0)),
                  pl.BlockSpec((1, 6, d), mod_map),
                  _const_spec((1, d)), _const_spec(win.shape), _const_spec(wkr.shape),
                  _const_spec(qg.shape), _const_spec(kvg.shape),
                  _const_spec(wqt.shape), _const_spec(wkn.shape), _const_spec(wvt.shape),
                  pl.BlockSpec((16, tm), lambda b, i: (0, i)),
                  pl.BlockSpec((16, tm), lambda b, i: (0, i)),
                  pl.BlockSpec((tm, 128), lambda b, i: (i, 0)),
                  pl.BlockSpec((tm, 128), lambda b, i: (i, 0)),
                  pl.BlockSpec((tm, 128), lambda b, i: (i, 0))],
        out_specs=out_specs,
        out_shape=out_shape,
        compiler_params=_params(2),
        name="mla_project",
    )(x, mod, g, win, wkr, qg, kvg, wqt, wkn, wvt, cf, sf, ct, s1, s2)


C_HEADS_PER_STEP = 4


def _mla_attn_kernel(n_parts, qt_ref, *refs):
    k_parts, v_parts, (o_ref, acc_ref, slab_ref, s_ref) = _split_parts(refs, n_parts)
    chains = [(qt_ref[0, C_SLAB * j:C_SLAB * (j + 1), :], slice(C_SLAB * j, C_SLAB * (j + 1)),
               slice(C_V_AUG * j, C_V_AUG * (j + 1))) for j in range(C_HEADS_PER_STEP)]
    tiles, uniform = _key_sequence(k_parts, v_parts)
    outs = _attend_pipelined(chains, tiles, acc_ref, s_ref, C_V, uniform=uniform)
    for j in range(C_HEADS_PER_STEP):
        slab_ref[C_V * j:C_V * (j + 1), :] = outs[j]
    o_ref[0] = slab_ref[...].T.astype(BF16)


def _mla_attention(qt, k_parts, v_parts, tq):
    bsz, _, rows = qt.shape
    per = C_HEADS_PER_STEP
    in_specs = [pl.BlockSpec((1, per * C_SLAB, tq), lambda b, p, i: (b, p, i))]
    for k in k_parts:
        in_specs.append(pl.BlockSpec((1, k.shape[1], per * C_SLAB), lambda b, p, i: (b, 0, p)))
    for v in v_parts:
        in_specs.append(pl.BlockSpec((1, v.shape[1], per * C_V_AUG, v.shape[3]), lambda b, p, i: (b, 0, p, 0)))
    return pl.pallas_call(
        functools.partial(_mla_attn_kernel, len(k_parts)),
        grid=(bsz, C_HEADS // per, rows // tq),
        in_specs=in_specs,
        out_specs=pl.BlockSpec((1, tq, per * C_V), lambda b, p, i: (b, i, p)),
        out_shape=jax.ShapeDtypeStruct((bsz, rows, C_HEADS * C_V), BF16),
        scratch_shapes=[pltpu.VMEM((per, C_V_AUG, tq), F32), pltpu.VMEM((per * C_V, tq), F32),
                        pltpu.VMEM((PIPE_SLOTS, per, v_parts[0].shape[3], tq), F32)],
        compiler_params=_params(3),
        name="mla_attention",
    )(qt, *k_parts, *v_parts)


def _mix_ffn_kernel(n_o, *refs):
    o_refs = refs[:n_o]
    x_ref, mod_ref, g_ref, wout_ref, w13_ref, w2_ref, out_ref = refs[n_o:]
    y = None
    row = 0
    for o_ref in o_refs:
        width = o_ref.shape[2]
        part = _dot(o_ref[0], wout_ref[row:row + width, :])
        y = part if y is None else y + part
        row += width
    x1 = x_ref[0] + mod_ref[0, 2:3, :] * _rms(y, g_ref[1:2, :])
    h = _rms(x1, g_ref[2:3, :]) * (1.0 + mod_ref[0, 4:5, :]) + mod_ref[0, 3:4, :]
    hb = h.astype(BF16)
    f = None
    for c in range(FFN_HIDDEN // FFN_CHUNK):
        lo = FFN_CHUNK * c
        gate = _dot(hb, w13_ref[:, lo:lo + FFN_CHUNK])
        up = _dot(hb, w13_ref[:, FFN_HIDDEN + lo:FFN_HIDDEN + lo + FFN_CHUNK])
        act = (gate / (1.0 + jnp.exp(-gate)) * up).astype(BF16)
        part = _dot(act, w2_ref[lo:lo + FFN_CHUNK, :])
        f = part if f is None else f + part
    out_ref[0] = x1 + mod_ref[0, 5:6, :] * _rms(f, g_ref[3:4, :])


def _mix_ffn(o_parts, x, mod, g4, wout, w13, w2, tm):
    bsz, rows, d = x.shape
    mod_map = (lambda b, i: (b, 0, 0)) if mod.shape[0] == bsz else (lambda b, i: (0, 0, 0))
    in_specs = [pl.BlockSpec((1, tm, o.shape[2]), lambda b, i: (b, i, 0)) for o in o_parts]
    in_specs += [pl.BlockSpec((1, tm, d), lambda b, i: (b, i, 0)),
                 pl.BlockSpec((1, 6, d), mod_map),
                 _const_spec(g4.shape), _const_spec(wout.shape), _const_spec(w13.shape), _const_spec(w2.shape)]
    return pl.pallas_call(
        functools.partial(_mix_ffn_kernel, len(o_parts)),
        grid=(bsz, rows // tm),
        in_specs=in_specs,
        out_specs=pl.BlockSpec((1, tm, d), lambda b, i: (b, i, 0)),
        out_shape=jax.ShapeDtypeStruct((bsz, rows, d), F32),
        compiler_params=_params(2),
        name="mix_ffn",
    )(*o_parts, x, mod, g4, wout, w13, w2)


def _rope_tables(n, rot_dim):
    pos = jnp.arange(n, dtype=jnp.int32)
    row = (pos // GRID_W).astype(F32)
    col = (pos % GRID_W).astype(F32)
    axis_dim = rot_dim // 2
    inv_freq = ROPE_THETA ** (-jnp.arange(0, axis_dim, 2, dtype=F32) / axis_dim)
    ang = jnp.concatenate([row[:, None] * inv_freq, col[:, None] * inv_freq], axis=-1)
    return jnp.cos(ang), jnp.sin(ang)


def _lane_tables(cos, sin, lane_dims, quarter):
    dims = jnp.asarray(lane_dims, dtype=jnp.int32)
    live = dims >= 0
    safe = jnp.where(live, dims, 0)
    axis, within = safe // (2 * quarter), safe % (2 * quarter)
    idx = axis * quarter + within % quarter
    first = within < quarter
    c = jnp.where(live[None, :], cos[:, idx], 1.0)
    s = jnp.where(live[None, :], sin[:, idx], 0.0)
    return c, jnp.where(first[None, :], -s, 0.0), jnp.where(first[None, :], 0.0, s)


def _identity_tables(n, half):
    return (jnp.ones((half, n), F32), jnp.zeros((half, n), F32),
            jnp.ones((n, 128), F32), jnp.zeros((n, 128), F32), jnp.zeros((n, 128), F32))


def kernel(x, c, ctx, c_ctx, ada_w, ada_b, norm_g, ffn_w13, ffn_w2, ab_w_in, ab_w_out, ab_sink, diff_lambda,
           diff_subln_g, mla_w_in, mla_q_norm_g, mla_kv_norm_g, mla_wq_b, mla_wkv_b, mla_w_out):
    bsz, n, d = x.shape
    n_ctx = ctx.shape[1]
    depth = ada_w.shape[0]
    assert n_ctx == KEY_TILE and n % Q_TILE == 0 and n % ROW_TILE == 0 and bsz + 1 <= 16

    cond = jnp.zeros((16, d), F32).at[:bsz].set(c).at[bsz].set(c_ctx)
    mods = _ada_modulation(cond, ada_w, ada_b).reshape(depth, 16, 6, d)

    cos_ab, sin_ab = _rope_tables(n, A_HEAD_DIM)
    ab_lane = [l % 64 for l in range(128)]
    tabs_ab = (cos_ab.T, sin_ab.T) + _lane_tables(cos_ab, sin_ab, ab_lane, 16)
    cos_c, sin_c = _rope_tables(n, C_ROPE)
    c_lane = [l - C_NOPE if C_NOPE <= l < C_NOPE + C_ROPE else -1 for l in range(128)]
    tabs_c = (cos_c.T, sin_c.T) + _lane_tables(cos_c, sin_c, c_lane, C_ROPE // 4)

    for l in range(depth):
        last = l == depth - 1
        mod_x, mod_c = mods[l, :bsz], mods[l, bsz:bsz + 1]
        g_in = norm_g[l, 0:1]
        w13, w2 = ffn_w13[l].astype(BF16), ffn_w2[l].astype(BF16)
        if l % 2 == 0:
            e = l // 2
            w = ab_w_in[e]
            o1, o2 = AB_Q_W + A_KV_W, AB_Q_W + 2 * A_KV_W
            o3 = o2 + B_QK_W
            wqt = w[:, :AB_Q_W].T.astype(BF16)
            wk = jnp.concatenate([w[:, AB_Q_W:o1], w[:, o2:o3]], axis=1).astype(BF16)
            wvt = jnp.concatenate([w[:, o1:o2], w[:, o3:]], axis=1).T.astype(BF16)
            wout = ab_w_out[e].astype(BF16)
            lam_init = 0.8 - 0.6 * math.exp(-0.3 * l)
            qt_x, ka_x, kb_x, va_x, vb_x = _ab_project(x, mod_x, g_in, wqt, wk, wvt, tabs_ab, ROW_TILE)
            qt_c, ka_c, kb_c, va_c, vb_c = _ab_project(ctx, mod_c, g_in, wqt, wk, wvt,
                                                       _identity_tables(n_ctx, 32), n_ctx)
            lam_args = (diff_lambda[e], diff_subln_g[e], lam_init)
            oa = _window_attention(qt_x, [ka_x, ka_c], [va_x, va_c], ab_sink[e], True, Q_TILE)
            ob = _diff_attention(qt_x, [kb_x, kb_c], [vb_x, vb_c], *lam_args, Q_TILE)
            o_x = [oa, ob]
            if not last:
                oca = _window_attention(qt_c, [ka_c], [va_c], ab_sink[e], False, n_ctx)
                ocb = _diff_attention(qt_c, [kb_c], [vb_c], *lam_args, n_ctx)
                o_c = [oca, ocb]
        else:
            o = l // 2
            w = mla_w_in[o]
            win = w[:, :C_LAT_W].astype(BF16)
            wkr = jnp.zeros((d, C_SLAB), F32).at[:, C_NOPE:C_NOPE + C_ROPE].set(w[:, C_LAT_W:]).astype(BF16)
            wq = mla_wq_b[o].reshape(C_Q_LORA, C_HEADS, C_NOPE + C_ROPE)
            wq = jnp.pad(wq, ((0, 0), (0, 0), (0, C_SLAB - C_NOPE - C_ROPE)))
            wqt = wq.reshape(C_Q_LORA, C_HEADS * C_SLAB).T.astype(BF16)
            wkv = mla_wkv_b[o].reshape(C_KV_LORA, C_HEADS, C_NOPE + C_V)
            wkn = jnp.pad(wkv[:, :, :C_NOPE], ((0, 0), (0, 0), (0, C_SLAB - C_NOPE)))
            wkn = wkn.reshape(C_KV_LORA, C_HEADS * C_SLAB).astype(BF16)
            wvt = wkv[:, :, C_NOPE:].reshape(C_KV_LORA, C_HEADS * C_V).T.astype(BF16)
            weights = (win, wkr, mla_q_norm_g[o:o + 1], mla_kv_norm_g[o:o + 1], wqt, wkn, wvt)
            wout = mla_w_out[o].astype(BF16)
            qt_x, k_x, vt_x = _mla_project(x, mod_x, g_in, weights, tabs_c, True, ROW_TILE)
            ident = _identity_tables(n_ctx, 16)
            if last:
                k_c, vt_c = _mla_project(ctx, mod_c, g_in, weights, ident, False, n_ctx)
            else:
                qt_c, k_c, vt_c = _mla_project(ctx, mod_c, g_in, weights, ident, True, n_ctx)
            o_x = [_mla_attention(qt_x, [k_x, k_c], [vt_x, vt_c], Q_TILE)]
            if not last:
                o_c = [_mla_attention(qt_c, [k_c], [vt_c], n_ctx)]
        x = _mix_ffn(o_x, x, mod_x, norm_g[l], wout, w13, w2, ROW_TILE)
        if not last:
            ctx = _mix_ffn(o_c, ctx, mod_c, norm_g[l], wout, w13, w2, n_ctx)
    return x
```

```python
import functools
import math
from typing import NamedTuple

import jax
import jax.numpy as jnp
from jax import lax
from jax.experimental import pallas as pl
from jax.experimental.pallas import tpu as pltpu

F32 = jnp.float32
BF16 = jnp.bfloat16

D_MODEL = 1024
GRID_W = 64
ROPE_THETA = 10000.0
NORM_EPS = 1e-6
NEG_INF = -1e30
WINDOW = 128
LOG2E = 1.4426950408889634

A_HEADS, A_KV_HEADS, A_HEAD_DIM = 8, 2, 64
B_HEADS, B_QK_DIM, B_V_DIM = 4, 64, 128
A_Q_W = A_HEADS * A_HEAD_DIM
A_KV_W = A_KV_HEADS * A_HEAD_DIM
B_QK_W = B_HEADS * 2 * B_QK_DIM
B_V_W = B_HEADS * B_V_DIM
AB_Q_W = A_Q_W + B_QK_W
AB_K_W = A_KV_W + B_QK_W
AB_V_W = A_KV_W + B_V_W

C_HEADS, C_Q_LORA, C_KV_LORA, C_NOPE, C_ROPE, C_V = 16, 384, 256, 64, 32, 64
C_SLAB = 128
C_LAT_W = C_Q_LORA + C_KV_LORA

FFN_HIDDEN = 2816
FFN_CHUNK = 256

ONES_ROWS = 16
A_V_AUG = A_HEAD_DIM + ONES_ROWS
B_V_AUG = B_V_DIM + ONES_ROWS
C_V_AUG = C_V + ONES_ROWS

KEY_TILE = 256
LAT_KEY_TILE = 512
PIPE_AHEAD = 2
PIPE_SLOTS = PIPE_AHEAD + 1
PIPE_GROUPS = 2
Q_TILES_PER_STEP = 2
Q_TILE = 512
ROW_TILE = 512
VMEM_LIMIT = 56 * 1024 * 1024

_NT = (((1,), (1,)), ((), ()))


def _dot(a, b):
    return jnp.dot(a, b, preferred_element_type=F32)


def _dot_nt(a, b):
    return lax.dot_general(a, b, _NT, preferred_element_type=F32)


def _rms(x, g):
    return x * lax.rsqrt(jnp.mean(x * x, axis=-1, keepdims=True) + NORM_EPS) * g


def _const_spec(shape):
    zeros = (0,) * len(shape)
    return pl.BlockSpec(shape, lambda *_: zeros, pipeline_mode=pl.Buffered(1))


def _params(n_axes):
    return pltpu.CompilerParams(dimension_semantics=("arbitrary",) * n_axes,
                                vmem_limit_bytes=VMEM_LIMIT)


def _ada_kernel(c_ref, w_ref, b_ref, o_ref):
    c = c_ref[...]
    s = c / (1.0 + jnp.exp(-c))
    o_ref[0] = _dot(s.astype(BF16), w_ref[0].astype(BF16)) + b_ref[0]


def _ada_modulation(cond, ada_w, ada_b):
    depth, d, n = ada_w.shape
    tn = 1536
    return pl.pallas_call(
        _ada_kernel,
        grid=(depth, n // tn),
        in_specs=[pl.BlockSpec((16, d), lambda l, j: (0, 0)),
                  pl.BlockSpec((1, d, tn), lambda l, j: (l, 0, j)),
                  pl.BlockSpec((1, 1, tn), lambda l, j: (l, 0, j))],
        out_specs=pl.BlockSpec((1, 16, tn), lambda l, j: (l, 0, j)),
        out_shape=jax.ShapeDtypeStruct((depth, 16, n), F32),
        compiler_params=_params(2),
        name="ada_modulation",
    )(cond, ada_w, ada_b.reshape(depth, 1, n))


def _rope_rows(blk, quarter, cf, sf):
    a, b = blk[0:quarter], blk[quarter:2 * quarter]
    c, d = blk[2 * quarter:3 * quarter], blk[3 * quarter:4 * quarter]
    cr, sr = cf[0:quarter], sf[0:quarter]
    cc, sc = cf[quarter:2 * quarter], sf[quarter:2 * quarter]
    return jnp.concatenate([a * cr - b * sr, b * cr + a * sr, c * cc - d * sc, d * cc + c * sc], axis=0)


def _rope_lanes(x, quarter, ct, s1, s2):
    return x * ct + pltpu.roll(x, 128 - quarter, 1) * s1 + pltpu.roll(x, quarter, 1) * s2


def _store_with_ones(vt_ref, tile, heads, dv):
    row = lax.broadcasted_iota(jnp.int32, (ONES_ROWS, tile.shape[1]), 0)
    ones_blk = jnp.where(row == 0, 1.0, 0.0).astype(BF16)
    aug = dv + ONES_ROWS
    for h in range(heads):
        vt_ref[aug * h:aug * h + dv, :] = tile[dv * h:dv * (h + 1)]
        vt_ref[aug * h + dv:aug * (h + 1), :] = ones_blk


def _ab_proj_kernel(x_ref, mod_ref, g_ref, wqt_ref, wk_ref, wvt_ref, cf_ref, sf_ref, ct_ref, s1_ref, s2_ref,
                    qt_ref, ka_ref, kb_ref, vta_ref, vtb_ref):
    h = _rms(x_ref[0], g_ref[...]) * (1.0 + mod_ref[0, 1:2, :]) + mod_ref[0, 0:1, :]
    hb = h.astype(BF16)
    cf, sf = cf_ref[...], sf_ref[...]
    qt = _dot_nt(wqt_ref[...], hb)
    for g in range(AB_Q_W // 64):
        blk = _rope_rows(qt[64 * g:64 * g + 64], 16, cf, sf)
        qt_ref[0, 64 * g:64 * g + 64, :] = (blk * (LOG2E * 64 ** -0.5)).astype(BF16)
    kk = _dot(hb, wk_ref[...])
    ct, s1, s2 = ct_ref[...], s1_ref[...], s2_ref[...]
    ka_ref[0] = _rope_lanes(kk[:, :A_KV_W], 16, ct, s1, s2).astype(BF16)
    for c in range(B_QK_W // 128):
        lo = A_KV_W + 128 * c
        kb_ref[0, :, 128 * c:128 * c + 128] = _rope_lanes(kk[:, lo:lo + 128], 16, ct, s1, s2).astype(BF16)
    vt = _dot_nt(wvt_ref[...], hb).astype(BF16)
    tka, tkb = vta_ref.shape[3], vtb_ref.shape[3]
    for j in range(vt.shape[1] // tka):
        _store_with_ones(vta_ref.at[0, j], vt[:A_KV_W, tka * j:tka * (j + 1)], A_KV_HEADS, A_HEAD_DIM)
    for j in range(vt.shape[1] // tkb):
        _store_with_ones(vtb_ref.at[0, j], vt[A_KV_W:, tkb * j:tkb * (j + 1)], B_HEADS, B_V_DIM)


def _ab_project(x, mod, g, wqt, wk, wvt, tabs, tm):
    bsz, rows, d = x.shape
    cf, sf, ct, s1, s2 = tabs
    mod_map = (lambda b, i: (b, 0, 0)) if mod.shape[0] == bsz else (lambda b, i: (0, 0, 0))
    tkb = min(tm, LAT_KEY_TILE)
    return pl.pallas_call(
        _ab_proj_kernel,
        grid=(bsz, rows // tm),
        in_specs=[pl.BlockSpec((1, tm, d), lambda b, i: (b, i, 0)),
                  pl.BlockSpec((1, 6, d), mod_map),
                  _const_spec((1, d)), _const_spec(wqt.shape), _const_spec(wk.shape), _const_spec(wvt.shape),
                  pl.BlockSpec((32, tm), lambda b, i: (0, i)),
                  pl.BlockSpec((32, tm), lambda b, i: (0, i)),
                  pl.BlockSpec((tm, 128), lambda b, i: (i, 0)),
                  pl.BlockSpec((tm, 128), lambda b, i: (i, 0)),
                  pl.BlockSpec((tm, 128), lambda b, i: (i, 0))],
        out_specs=[pl.BlockSpec((1, AB_Q_W, tm), lambda b, i: (b, 0, i)),
                   pl.BlockSpec((1, tm, A_KV_W), lambda b, i: (b, i, 0)),
                   pl.BlockSpec((1, tm, B_QK_W), lambda b, i: (b, i, 0)),
                   pl.BlockSpec((1, tm // KEY_TILE, A_KV_HEADS * A_V_AUG, KEY_TILE), lambda b, i: (b, i, 0, 0)),
                   pl.BlockSpec((1, tm // tkb, B_HEADS * B_V_AUG, tkb), lambda b, i: (b, i, 0, 0))],
        out_shape=[jax.ShapeDtypeStruct((bsz, AB_Q_W, rows), BF16),
                   jax.ShapeDtypeStruct((bsz, rows, A_KV_W), BF16),
                   jax.ShapeDtypeStruct((bsz, rows, B_QK_W), BF16),
                   jax.ShapeDtypeStruct((bsz, rows // KEY_TILE, A_KV_HEADS * A_V_AUG, KEY_TILE), BF16),
                   jax.ShapeDtypeStruct((bsz, rows // tkb, B_HEADS * B_V_AUG, tkb), BF16)],
        compiler_params=_params(2),
        name="ab_project",
    )(x, mod, g, wqt, wk, wvt, cf, sf, ct, s1, s2)


def _tile_start(t, size):
    return t * size if isinstance(t, int) else pl.multiple_of(t * size, size)


def _normalise(acc, dv):
    return acc[:dv] * (1.0 / acc[dv:dv + 1])


class _KeyTile(NamedTuple):
    k_ref: object
    v_ref: object
    index: object
    offset: int
    size: int
    valid: object = None


def _key_sequence(k_parts, v_parts):
    if len(k_parts) == 1:
        assert v_parts[0].shape[1] == 1
        return [_KeyTile(k_parts[0], v_parts[0], 0, 0, v_parts[0].shape[3])], None
    (k_lat, k_ctx), (v_lat, v_ctx) = k_parts, v_parts
    n_lat, tk = v_lat.shape[1], v_lat.shape[3]
    assert v_ctx.shape[1] == 1 and n_lat >= 3 and tk % 256 == 0
    halves = lambda i: [_KeyTile(k_lat, v_lat, i, 0, tk // 2), _KeyTile(k_lat, v_lat, i, tk // 2, tk // 2)]
    tiles = (halves(0) + [_KeyTile(k_lat, v_lat, i, 0, tk) for i in range(1, n_lat - 1)] + halves(n_lat - 1)
             + [_KeyTile(k_ctx, v_ctx, 0, 0, v_ctx.shape[3])])
    return tiles, (2, n_lat)


class _Segment(NamedTuple):
    chains: list
    tiles: list
    acc_ref: object
    uniform: object = None
    m0: object = None


def _attend_pipelined(segments, s_ref, dv):
    n = len(segments[0].chains)
    tq = segments[0].chains[0][0].shape[1]
    bases = [sum(len(seg.tiles) for seg in segments[:g]) for g in range(len(segments))]
    n_pos = bases[-1] + len(segments[-1].tiles)
    assert s_ref.shape[0] == PIPE_SLOTS and all(seg.acc_ref.shape[1] > dv for seg in segments)
    running = []
    for seg in segments:
        if seg.m0 is None:
            seg.acc_ref[...] = jnp.zeros(seg.acc_ref.shape, F32)
            running.append([jnp.full((1, tq), NEG_INF, F32) for _ in range(n)])
        else:
            row = lax.broadcasted_iota(jnp.int32, seg.acc_ref.shape, 1)
            seg.acc_ref[...] = jnp.where(row == dv, 1.0, 0.0)
            running.append(list(seg.m0))

    def locate(g, p):
        seg = segments[g]
        if isinstance(p, int):
            return seg.tiles[p]
        lo = seg.uniform[0]
        return seg.tiles[lo]._replace(index=p - (lo - seg.tiles[lo].index))

    def split(position):
        g = max(g for g in range(len(segments)) if bases[g] <= position)
        return g, position - bases[g]

    def park(g, p, slot, c):
        tile = locate(g, p)
        q_pad, lanes, _ = segments[g].chains[c]
        start = _tile_start(tile.index, tile.v_ref.shape[3]) + tile.offset
        s = _dot(tile.k_ref[0, pl.ds(start, tile.size), lanes], q_pad)
        if tile.valid is not None:
            s = jnp.where(tile.valid, s, NEG_INF)
        s_ref[slot, c, 0:tile.size, :] = s
        return jnp.max(s, axis=0, keepdims=True)

    def finish(g, p, slot, c, m, tile_max):
        tile, acc_ref = locate(g, p), segments[g].acc_ref
        m_new = jnp.maximum(m, tile_max)
        prob = jnp.exp2(s_ref[slot, c, 0:tile.size, :] - m_new)
        v_tile = tile.v_ref[0, tile.index, segments[g].chains[c][2], tile.offset:tile.offset + tile.size]
        acc_ref[c] = jnp.exp2(m - m_new) * acc_ref[c] + _dot(v_tile, prob.astype(BF16))
        return m_new

    def sweep(g, p, slot, ms, maxima, ahead):
        ahead_slot = (slot + PIPE_AHEAD) % PIPE_SLOTS
        for c in range(n):
            if ahead is not None:
                maxima[ahead_slot][c] = park(ahead[0], ahead[1], ahead_slot, c)
            ms[c] = finish(g, p, slot, c, ms[c], maxima[slot][c])

    def static_sweeps(g, lo, hi, ms, maxima):
        for p in range(lo, hi):
            position = bases[g] + p
            ahead = split(position + PIPE_AHEAD) if position + PIPE_AHEAD < n_pos else None
            sweep(g, p, position % PIPE_SLOTS, ms, maxima, ahead)

    def unpack(carry):
        return list(carry[:n]), [list(carry[n * (1 + s):n * (2 + s)]) for s in range(PIPE_SLOTS)]

    per_iter = PIPE_SLOTS * PIPE_GROUPS
    maxima = [[jnp.full((1, tq), NEG_INF, F32) for _ in range(n)] for _ in range(PIPE_SLOTS)]
    for position in range(min(PIPE_AHEAD, n_pos)):
        maxima[position] = [park(*split(position), position, c) for c in range(n)]
    for g, seg in enumerate(segments):
        loop_lo, n_iters = 0, 0
        if seg.uniform is not None:
            loop_lo, n_iters = seg.uniform[0], (seg.uniform[1] - seg.uniform[0] - PIPE_AHEAD) // per_iter

        def group(j, carry, g=g, loop_lo=loop_lo):
            ms, maxima = unpack(carry)
            for u in range(per_iter):
                p = loop_lo + per_iter * j + u
                sweep(g, p, (bases[g] + loop_lo + u) % PIPE_SLOTS, ms, maxima, (g, p + PIPE_AHEAD))
            return tuple(ms + sum(maxima, []))

        ms = running[g]
        static_sweeps(g, 0, loop_lo, ms, maxima)
        if n_iters:
            ms, maxima = unpack(lax.fori_loop(0, n_iters, group, tuple(ms + sum(maxima, []))))
        static_sweeps(g, loop_lo + per_iter * n_iters, len(seg.tiles), ms, maxima)
    return [[_normalise(seg.acc_ref[c], dv) for c in range(n)] for seg in segments]


def _split_parts(refs, n_parts):
    return refs[:n_parts], refs[n_parts:2 * n_parts], refs[2 * n_parts:]


B_HEADS_PER_STEP = 2


def _diff_attn_kernel(lam_init, n_parts, qt_ref, *refs):
    k_parts, v_parts, (lam_ref, g_ref, o_ref, acc_ref, s_ref) = _split_parts(refs, n_parts)
    n_seg = acc_ref.shape[0]
    tq = qt_ref.shape[2] // n_seg
    tiles, uniform = _key_sequence(k_parts, v_parts)
    segments = []
    for g in range(n_seg):
        chains = []
        for j in range(B_HEADS_PER_STEP):
            q = qt_ref[0, 128 * j:128 * (j + 1), tq * g:tq * (g + 1)]
            row = lax.broadcasted_iota(jnp.int32, q.shape, 0)
            zero = jnp.zeros_like(q)
            slab, vrows = slice(128 * j, 128 * (j + 1)), slice(B_V_AUG * j, B_V_AUG * (j + 1))
            chains.append((jnp.where(row < B_QK_DIM, q, zero), slab, vrows))
            chains.append((jnp.where(row >= B_QK_DIM, q, zero), slab, vrows))
        segments.append(_Segment(chains, tiles, acc_ref.at[g], uniform))
    outs = _attend_pipelined(segments, s_ref, B_V_DIM)
    lv = lam_ref[...]
    lam = (jnp.exp(jnp.sum(lv[0:1] * lv[1:2], axis=1, keepdims=True))
           - jnp.exp(jnp.sum(lv[2:3] * lv[3:4], axis=1, keepdims=True)) + lam_init)
    for g in range(n_seg):
        for j in range(B_HEADS_PER_STEP):
            dlt = outs[g][2 * j] - lam * outs[g][2 * j + 1]
            y = (dlt * lax.rsqrt(jnp.mean(dlt * dlt, axis=0, keepdims=True) + NORM_EPS) * g_ref[...]
                 * (1.0 - lam_init))
            o_ref[0, tq * g:tq * (g + 1), B_V_DIM * j:B_V_DIM * (j + 1)] = y.T.astype(BF16)


def _query_tiles_per_step(rows, tq):
    return Q_TILES_PER_STEP if rows % (Q_TILES_PER_STEP * tq) == 0 else 1


def _diff_attention(qt, k_parts, v_parts, lam_vec, subln_g, lam_init, tq):
    bsz, _, rows = qt.shape
    per = B_HEADS_PER_STEP
    n_seg = _query_tiles_per_step(rows, tq)
    q_off = A_Q_W // (128 * per)
    in_specs = [pl.BlockSpec((1, 128 * per, tq * n_seg), lambda b, h, i: (b, q_off + h, i))]
    for k in k_parts:
        in_specs.append(pl.BlockSpec((1, k.shape[1], 128 * per), lambda b, h, i: (b, 0, h)))
    for v in v_parts:
        in_specs.append(pl.BlockSpec((1, v.shape[1], B_V_AUG * per, v.shape[3]), lambda b, h, i: (b, 0, h, 0)))
    in_specs += [pl.BlockSpec(lam_vec.shape, lambda b, h, i: (0, 0)),
                 pl.BlockSpec((B_V_DIM, 1), lambda b, h, i: (0, 0))]
    return pl.pallas_call(
        functools.partial(_diff_attn_kernel, lam_init, len(k_parts)),
        grid=(bsz, B_HEADS // per, rows // (tq * n_seg)),
        in_specs=in_specs,
        out_specs=pl.BlockSpec((1, tq * n_seg, B_V_DIM * per), lambda b, h, i: (b, i, h)),
        out_shape=jax.ShapeDtypeStruct((bsz, rows, B_V_W), BF16),
        scratch_shapes=[pltpu.VMEM((n_seg, 2 * per, B_V_AUG, tq), F32),
                        pltpu.VMEM((PIPE_SLOTS, 2 * per, v_parts[0].shape[3], tq), F32)],
        compiler_params=_params(3),
        name="diff_attention",
    )(qt, *k_parts, *v_parts, lam_vec, subln_g.reshape(B_V_DIM, 1))


def _window_attn_kernel(windowed, n_parts, sink_ref, qt_ref, *refs):
    k_parts, v_parts, (o_ref, acc_ref, slab_ref, s_ref) = _split_parts(refs, n_parts)
    tq = qt_ref.shape[2]
    i = pl.program_id(1)
    group = A_HEADS // A_KV_HEADS
    zeros = jnp.zeros((A_HEAD_DIM, tq), BF16)
    if not windowed:
        tiles, _ = _key_sequence(k_parts, v_parts)
    else:
        assert 2 * WINDOW == KEY_TILE and tq % KEY_TILE == 0
        per_q, n_tiles = tq // KEY_TILE, v_parts[0].shape[1]
        first = i * per_q - 1
        rel = (lax.broadcasted_iota(jnp.int32, (KEY_TILE, tq), 1)
               - lax.broadcasted_iota(jnp.int32, (KEY_TILE, tq), 0))

        def window_tile(u, offset, size):
            t = first + u
            reach = jnp.where((t >= 0) & (t < n_tiles), WINDOW, -1)
            valid = jnp.abs(rel[:size] + (i * tq - t * KEY_TILE - offset)) <= reach
            return _KeyTile(k_parts[0], v_parts[0], jnp.clip(t, 0, n_tiles - 1), offset, size, valid)

        tiles = ([window_tile(0, WINDOW, WINDOW)] + [window_tile(u, 0, KEY_TILE) for u in range(1, per_q + 1)]
                 + [window_tile(per_q + 1, 0, WINDOW), _KeyTile(k_parts[1], v_parts[1], 0, 0, KEY_TILE)])
    for kv in range(A_KV_HEADS):
        rows = slice(A_V_AUG * kv, A_V_AUG * (kv + 1))
        chains, m0 = [], []
        for hq in range(group * kv, group * (kv + 1)):
            qh = qt_ref[0, A_HEAD_DIM * hq:A_HEAD_DIM * (hq + 1), :]
            chains.append((jnp.concatenate([qh, zeros] if kv == 0 else [zeros, qh], axis=0), slice(None), rows))
            m0.append(jnp.full((1, tq), sink_ref[hq] * LOG2E, F32))
        outs, = _attend_pipelined([_Segment(chains, tiles, acc_ref.at[kv], None, m0)], s_ref.at[kv], A_HEAD_DIM)
        for c in range(group):
            slab_ref[kv, A_HEAD_DIM * c:A_HEAD_DIM * (c + 1), :] = outs[c]
        width = group * A_HEAD_DIM
        o_ref[0, :, width * kv:width * (kv + 1)] = slab_ref[kv].T.astype(BF16)


def _window_attention(qt, k_parts, v_parts, sink, windowed, tq):
    bsz, _, rows = qt.shape
    in_specs = [pl.BlockSpec(memory_space=pltpu.SMEM),
                pl.BlockSpec((1, A_Q_W, tq), lambda b, i: (b, 0, i))]
    for k in k_parts:
        in_specs.append(pl.BlockSpec((1, k.shape[1], 128), lambda b, i: (b, 0, 0)))
    for v in v_parts:
        in_specs.append(pl.BlockSpec((1, v.shape[1], A_KV_HEADS * A_V_AUG, KEY_TILE), lambda b, i: (b, 0, 0, 0)))
    group = A_HEADS // A_KV_HEADS
    return pl.pallas_call(
        functools.partial(_window_attn_kernel, windowed, len(k_parts)),
        grid=(bsz, rows // tq),
        in_specs=in_specs,
        out_specs=pl.BlockSpec((1, tq, A_Q_W), lambda b, i: (b, i, 0)),
        out_shape=jax.ShapeDtypeStruct((bsz, rows, A_Q_W), BF16),
        scratch_shapes=[pltpu.VMEM((A_KV_HEADS, group, A_V_AUG, tq), F32),
                        pltpu.VMEM((A_KV_HEADS, group * A_HEAD_DIM, tq), F32),
                        pltpu.VMEM((A_KV_HEADS, PIPE_SLOTS, group, KEY_TILE, tq), F32)],
        compiler_params=_params(2),
        name="window_attention",
    )(sink, qt, *k_parts, *v_parts)


def _mla_proj_kernel(with_q, x_ref, mod_ref, g_ref, win_ref, wkr_ref, qg_ref, kvg_ref, wqt_ref, wkn_ref, wvt_ref,
                     cf_ref, sf_ref, ct_ref, s1_ref, s2_ref, *out_refs):
    h = _rms(x_ref[0], g_ref[...]) * (1.0 + mod_ref[0, 1:2, :]) + mod_ref[0, 0:1, :]
    hb = h.astype(BF16)
    lat = _dot(hb, win_ref[...])
    kvn = _rms(lat[:, C_Q_LORA:], kvg_ref[...]).astype(BF16)
    if with_q:
        qt_ref, k_ref, vt_ref = out_refs
        qn = _rms(lat[:, :C_Q_LORA], qg_ref[...]).astype(BF16)
        qt = _dot_nt(wqt_ref[...], qn)
        cf, sf = cf_ref[...], sf_ref[...]
        scale = LOG2E * (C_NOPE + C_ROPE) ** -0.5
        for hd in range(C_HEADS):
            base = C_SLAB * hd
            rope = _rope_rows(qt[base + C_NOPE:base + C_NOPE + C_ROPE], C_ROPE // 4, cf, sf)
            blk = jnp.concatenate([qt[base:base + C_NOPE], rope, qt[base + C_NOPE + C_ROPE:base + C_SLAB]], axis=0)
            qt_ref[0, base:base + C_SLAB, :] = (blk * scale).astype(BF16)
    else:
        k_ref, vt_ref = out_refs
    kn = _dot(kvn, wkn_ref[...])
    kr = _rope_lanes(_dot(hb, wkr_ref[...]), C_ROPE // 4, ct_ref[...], s1_ref[...], s2_ref[...])
    for hd in range(C_HEADS):
        k_ref[0, :, C_SLAB * hd:C_SLAB * (hd + 1)] = (kn[:, C_SLAB * hd:C_SLAB * (hd + 1)] + kr).astype(BF16)
    vt = _dot_nt(wvt_ref[...], kvn).astype(BF16)
    tk = vt_ref.shape[3]
    for j in range(vt.shape[1] // tk):
        _store_with_ones(vt_ref.at[0, j], vt[:, tk * j:tk * (j + 1)], C_HEADS, C_V)


def _mla_project(x, mod, g, weights, tabs, with_q, tm):
    bsz, rows, d = x.shape
    win, wkr, qg, kvg, wqt, wkn, wvt = weights
    cf, sf, ct, s1, s2 = tabs
    mod_map = (lambda b, i: (b, 0, 0)) if mod.shape[0] == bsz else (lambda b, i: (0, 0, 0))
    kw, vw, tk = C_HEADS * C_SLAB, C_HEADS * C_V_AUG, min(tm, LAT_KEY_TILE)
    out_specs = [pl.BlockSpec((1, tm, kw), lambda b, i: (b, i, 0)),
                 pl.BlockSpec((1, tm // tk, vw, tk), lambda b, i: (b, i, 0, 0))]
    out_shape = [jax.ShapeDtypeStruct((bsz, rows, kw), BF16),
                 jax.ShapeDtypeStruct((bsz, rows // tk, vw, tk), BF16)]
    if with_q:
        out_specs.insert(0, pl.BlockSpec((1, kw, tm), lambda b, i: (b, 0, i)))
        out_shape.insert(0, jax.ShapeDtypeStruct((bsz, kw, rows), BF16))
    return pl.pallas_call(
        functools.partial(_mla_proj_kernel, with_q),
        grid=(bsz, rows // tm),
        in_specs=[pl.BlockSpec((1, tm, d), lambda b, i: (b, i, 0)),
                  pl.BlockSpec((1, 6, d), mod_map),
                  _const_spec((1, d)), _const_spec(win.shape), _const_spec(wkr.shape),
                  _const_spec(qg.shape), _const_spec(kvg.shape),
                  _const_spec(wqt.shape), _const_spec(wkn.shape), _const_spec(wvt.shape),
                  pl.BlockSpec((16, tm), lambda b, i: (0, i)),
                  pl.BlockSpec((16, tm), lambda b, i: (0, i)),
                  pl.BlockSpec((tm, 128), lambda b, i: (i, 0)),
                  pl.BlockSpec((tm, 128), lambda b, i: (i, 0)),
                  pl.BlockSpec((tm, 128), lambda b, i: (i, 0))],
        out_specs=out_specs,
        out_shape=out_shape,
        compiler_params=_params(2),
        name="mla_project",
    )(x, mod, g, win, wkr, qg, kvg, wqt, wkn, wvt, cf, sf, ct, s1, s2)


C_HEADS_PER_STEP = 4


def _mla_attn_kernel(n_parts, qt_ref, *refs):
    k_parts, v_parts, (o_ref, acc_ref, slab_ref, s_ref) = _split_parts(refs, n_parts)
    n_seg = acc_ref.shape[0]
    tq = qt_ref.shape[2] // n_seg
    tiles, uniform = _key_sequence(k_parts, v_parts)
    segments = []
    for g in range(n_seg):
        chains = [(qt_ref[0, C_SLAB * j:C_SLAB * (j + 1), tq * g:tq * (g + 1)], slice(C_SLAB * j, C_SLAB * (j + 1)),
                   slice(C_V_AUG * j, C_V_AUG * (j + 1))) for j in range(C_HEADS_PER_STEP)]
        segments.append(_Segment(chains, tiles, acc_ref.at[g], uniform))
    outs = _attend_pipelined(segments, s_ref, C_V)
    for g in range(n_seg):
        for j in range(C_HEADS_PER_STEP):
            slab_ref[g, C_V * j:C_V * (j + 1), :] = outs[g][j]
        o_ref[0, tq * g:tq * (g + 1), :] = slab_ref[g].T.astype(BF16)


def _mla_attention(qt, k_parts, v_parts, tq):
    bsz, _, rows = qt.shape
    per = C_HEADS_PER_STEP
    n_seg = _query_tiles_per_step(rows, tq)
    in_specs = [pl.BlockSpec((1, per * C_SLAB, tq * n_seg), lambda b, p, i: (b, p, i))]
    for k in k_parts:
        in_specs.append(pl.BlockSpec((1, k.shape[1], per * C_SLAB), lambda b, p, i: (b, 0, p)))
    for v in v_parts:
        in_specs.append(pl.BlockSpec((1, v.shape[1], per * C_V_AUG, v.shape[3]), lambda b, p, i: (b, 0, p, 0)))
    return pl.pallas_call(
        functools.partial(_mla_attn_kernel, len(k_parts)),
        grid=(bsz, C_HEADS // per, rows // (tq * n_seg)),
        in_specs=in_specs,
        out_specs=pl.BlockSpec((1, tq * n_seg, per * C_V), lambda b, p, i: (b, i, p)),
        out_shape=jax.ShapeDtypeStruct((bsz, rows, C_HEADS * C_V), BF16),
        scratch_shapes=[pltpu.VMEM((n_seg, per, C_V_AUG, tq), F32), pltpu.VMEM((n_seg, per * C_V, tq), F32),
                        pltpu.VMEM((PIPE_SLOTS, per, v_parts[0].shape[3], tq), F32)],
        compiler_params=_params(3),
        name="mla_attention",
    )(qt, *k_parts, *v_parts)


def _mix_ffn_kernel(n_o, *refs):
    o_refs = refs[:n_o]
    x_ref, mod_ref, g_ref, wout_ref, w13_ref, w2_ref, out_ref = refs[n_o:]
    y = None
    row = 0
    for o_ref in o_refs:
        width = o_ref.shape[2]
        part = _dot(o_ref[0], wout_ref[row:row + width, :])
        y = part if y is None else y + part
        row += width
    x1 = x_ref[0] + mod_ref[0, 2:3, :] * _rms(y, g_ref[1:2, :])
    h = _rms(x1, g_ref[2:3, :]) * (1.0 + mod_ref[0, 4:5, :]) + mod_ref[0, 3:4, :]
    hb = h.astype(BF16)
    f = None
    for c in range(FFN_HIDDEN // FFN_CHUNK):
        lo = FFN_CHUNK * c
        gate = _dot(hb, w13_ref[:, lo:lo + FFN_CHUNK])
        up = _dot(hb, w13_ref[:, FFN_HIDDEN + lo:FFN_HIDDEN + lo + FFN_CHUNK])
        act = (gate / (1.0 + jnp.exp(-gate)) * up).astype(BF16)
        part = _dot(act, w2_ref[lo:lo + FFN_CHUNK, :])
        f = part if f is None else f + part
    out_ref[0] = x1 + mod_ref[0, 5:6, :] * _rms(f, g_ref[3:4, :])


def _mix_ffn(o_parts, x, mod, g4, wout, w13, w2, tm):
    bsz, rows, d = x.shape
    mod_map = (lambda b, i: (b, 0, 0)) if mod.shape[0] == bsz else (lambda b, i: (0, 0, 0))
    in_specs = [pl.BlockSpec((1, tm, o.shape[2]), lambda b, i: (b, i, 0)) for o in o_parts]
    in_specs += [pl.BlockSpec((1, tm, d), lambda b, i: (b, i, 0)),
                 pl.BlockSpec((1, 6, d), mod_map),
                 _const_spec(g4.shape), _const_spec(wout.shape), _const_spec(w13.shape), _const_spec(w2.shape)]
    return pl.pallas_call(
        functools.partial(_mix_ffn_kernel, len(o_parts)),
        grid=(bsz, rows // tm),
        in_specs=in_specs,
        out_specs=pl.BlockSpec((1, tm, d), lambda b, i: (b, i, 0)),
        out_shape=jax.ShapeDtypeStruct((bsz, rows, d), F32),
        compiler_params=_params(2),
        name="mix_ffn",
    )(*o_parts, x, mod, g4, wout, w13, w2)


def _rope_tables(n, rot_dim):
    pos = jnp.arange(n, dtype=jnp.int32)
    row = (pos // GRID_W).astype(F32)
    col = (pos % GRID_W).astype(F32)
    axis_dim = rot_dim // 2
    inv_freq = ROPE_THETA ** (-jnp.arange(0, axis_dim, 2, dtype=F32) / axis_dim)
    ang = jnp.concatenate([row[:, None] * inv_freq, col[:, None] * inv_freq], axis=-1)
    return jnp.cos(ang), jnp.sin(ang)


def _lane_tables(cos, sin, lane_dims, quarter):
    dims = jnp.asarray(lane_dims, dtype=jnp.int32)
    live = dims >= 0
    safe = jnp.where(live, dims, 0)
    axis, within = safe // (2 * quarter), safe % (2 * quarter)
    idx = axis * quarter + within % quarter
    first = within < quarter
    c = jnp.where(live[None, :], cos[:, idx], 1.0)
    s = jnp.where(live[None, :], sin[:, idx], 0.0)
    return c, jnp.where(first[None, :], -s, 0.0), jnp.where(first[None, :], 0.0, s)


def _identity_tables(n, half):
    return (jnp.ones((half, n), F32), jnp.zeros((half, n), F32),
            jnp.ones((n, 128), F32), jnp.zeros((n, 128), F32), jnp.zeros((n, 128), F32))


def kernel(x, c, ctx, c_ctx, ada_w, ada_b, norm_g, ffn_w13, ffn_w2, ab_w_in, ab_w_out, ab_sink, diff_lambda,
           diff_subln_g, mla_w_in, mla_q_norm_g, mla_kv_norm_g, mla_wq_b, mla_wkv_b, mla_w_out):
    bsz, n, d = x.shape
    n_ctx = ctx.shape[1]
    depth = ada_w.shape[0]
    assert n_ctx == KEY_TILE and n % Q_TILE == 0 and n % ROW_TILE == 0 and bsz + 1 <= 16

    cond = jnp.zeros((16, d), F32).at[:bsz].set(c).at[bsz].set(c_ctx)
    mods = _ada_modulation(cond, ada_w, ada_b).reshape(depth, 16, 6, d)

    cos_ab, sin_ab = _rope_tables(n, A_HEAD_DIM)
    ab_lane = [l % 64 for l in range(128)]
    tabs_ab = (cos_ab.T, sin_ab.T) + _lane_tables(cos_ab, sin_ab, ab_lane, 16)
    cos_c, sin_c = _rope_tables(n, C_ROPE)
    c_lane = [l - C_NOPE if C_NOPE <= l < C_NOPE + C_ROPE else -1 for l in range(128)]
    tabs_c = (cos_c.T, sin_c.T) + _lane_tables(cos_c, sin_c, c_lane, C_ROPE // 4)

    for l in range(depth):
        last = l == depth - 1
        mod_x, mod_c = mods[l, :bsz], mods[l, bsz:bsz + 1]
        g_in = norm_g[l, 0:1]
        w13, w2 = ffn_w13[l].astype(BF16), ffn_w2[l].astype(BF16)
        if l % 2 == 0:
            e = l // 2
            w = ab_w_in[e]
            o1, o2 = AB_Q_W + A_KV_W, AB_Q_W + 2 * A_KV_W
            o3 = o2 + B_QK_W
            wqt = w[:, :AB_Q_W].T.astype(BF16)
            wk = jnp.concatenate([w[:, AB_Q_W:o1], w[:, o2:o3]], axis=1).astype(BF16)
            wvt = jnp.concatenate([w[:, o1:o2], w[:, o3:]], axis=1).T.astype(BF16)
            wout = ab_w_out[e].astype(BF16)
            lam_init = 0.8 - 0.6 * math.exp(-0.3 * l)
            qt_x, ka_x, kb_x, va_x, vb_x = _ab_project(x, mod_x, g_in, wqt, wk, wvt, tabs_ab, ROW_TILE)
            qt_c, ka_c, kb_c, va_c, vb_c = _ab_project(ctx, mod_c, g_in, wqt, wk, wvt,
                                                       _identity_tables(n_ctx, 32), n_ctx)
            lam_args = (diff_lambda[e], diff_subln_g[e], lam_init)
            oa = _window_attention(qt_x, [ka_x, ka_c], [va_x, va_c], ab_sink[e], True, Q_TILE)
            ob = _diff_attention(qt_x, [kb_x, kb_c], [vb_x, vb_c], *lam_args, Q_TILE)
            o_x = [oa, ob]
            if not last:
                oca = _window_attention(qt_c, [ka_c], [va_c], ab_sink[e], False, n_ctx)
                ocb = _diff_attention(qt_c, [kb_c], [vb_c], *lam_args, n_ctx)
                o_c = [oca, ocb]
        else:
            o = l // 2
            w = mla_w_in[o]
            win = w[:, :C_LAT_W].astype(BF16)
            wkr = jnp.zeros((d, C_SLAB), F32).at[:, C_NOPE:C_NOPE + C_ROPE].set(w[:, C_LAT_W:]).astype(BF16)
            wq = mla_wq_b[o].reshape(C_Q_LORA, C_HEADS, C_NOPE + C_ROPE)
            wq = jnp.pad(wq, ((0, 0), (0, 0), (0, C_SLAB - C_NOPE - C_ROPE)))
            wqt = wq.reshape(C_Q_LORA, C_HEADS * C_SLAB).T.astype(BF16)
            wkv = mla_wkv_b[o].reshape(C_KV_LORA, C_HEADS, C_NOPE + C_V)
            wkn = jnp.pad(wkv[:, :, :C_NOPE], ((0, 0), (0, 0), (0, C_SLAB - C_NOPE)))
            wkn = wkn.reshape(C_KV_LORA, C_HEADS * C_SLAB).astype(BF16)
            wvt = wkv[:, :, C_NOPE:].reshape(C_KV_LORA, C_HEADS * C_V).T.astype(BF16)
            weights = (win, wkr, mla_q_norm_g[o:o + 1], mla_kv_norm_g[o:o + 1], wqt, wkn, wvt)
            wout = mla_w_out[o].astype(BF16)
            qt_x, k_x, vt_x = _mla_project(x, mod_x, g_in, weights, tabs_c, True, ROW_TILE)
            ident = _identity_tables(n_ctx, 16)
            if last:
                k_c, vt_c = _mla_project(ctx, mod_c, g_in, weights, ident, False, n_ctx)
            else:
                qt_c, k_c, vt_c = _mla_project(ctx, mod_c, g_in, weights, ident, True, n_ctx)
            o_x = [_mla_attention(qt_x, [k_x, k_c], [vt_x, vt_c], Q_TILE)]
            if not last:
                o_c = [_mla_attention(qt_c, [k_c], [vt_c], n_ctx)]
        x = _mix_ffn(o_x, x, mod_x, norm_g[l], wout, w13, w2, ROW_TILE)
        if not last:
            ctx = _mix_ffn(o_c, ctx, mod_c, norm_g[l], wout, w13, w2, n_ctx)
    return x
```

```python
import functools
import math
from typing import NamedTuple

import jax
import jax.numpy as jnp
from jax import lax
from jax.experimental import pallas as pl
from jax.experimental.pallas import tpu as pltpu

F32 = jnp.float32
BF16 = jnp.bfloat16

D_MODEL = 1024
GRID_W = 64
ROPE_THETA = 10000.0
NORM_EPS = 1e-6
NEG_INF = -1e30
WINDOW = 128
LOG2E = 1.4426950408889634

A_HEADS, A_KV_HEADS, A_HEAD_DIM = 8, 2, 64
B_HEADS, B_QK_DIM, B_V_DIM = 4, 64, 128
A_Q_W = A_HEADS * A_HEAD_DIM
A_KV_W = A_KV_HEADS * A_HEAD_DIM
B_QK_W = B_HEADS * 2 * B_QK_DIM
B_V_W = B_HEADS * B_V_DIM
AB_Q_W = A_Q_W + B_QK_W
AB_K_W = A_KV_W + B_QK_W
AB_V_W = A_KV_W + B_V_W

C_HEADS, C_Q_LORA, C_KV_LORA, C_NOPE, C_ROPE, C_V = 16, 384, 256, 64, 32, 64
C_SLAB = 128
C_LAT_W = C_Q_LORA + C_KV_LORA

FFN_HIDDEN = 2816
FFN_CHUNK = 256

ONES_ROWS = 16
A_V_AUG = A_HEAD_DIM + ONES_ROWS
B_V_AUG = B_V_DIM + ONES_ROWS
C_V_AUG = C_V + ONES_ROWS

KEY_TILE = 256
LAT_KEY_TILE = 512
PIPE_AHEAD = 2
PIPE_SLOTS = PIPE_AHEAD + 1
PIPE_GROUPS = 2
Q_TILES_PER_STEP = 4
Q_TILE = 512
ROW_TILE = 512
VMEM_LIMIT = 56 * 1024 * 1024

_NT = (((1,), (1,)), ((), ()))


def _dot(a, b):
    return jnp.dot(a, b, preferred_element_type=F32)


def _dot_nt(a, b):
    return lax.dot_general(a, b, _NT, preferred_element_type=F32)


def _rms(x, g):
    return x * lax.rsqrt(jnp.mean(x * x, axis=-1, keepdims=True) + NORM_EPS) * g


def _const_spec(shape):
    zeros = (0,) * len(shape)
    return pl.BlockSpec(shape, lambda *_: zeros, pipeline_mode=pl.Buffered(1))


def _params(n_axes):
    return pltpu.CompilerParams(dimension_semantics=("arbitrary",) * n_axes,
                                vmem_limit_bytes=VMEM_LIMIT)


def _ada_kernel(c_ref, w_ref, b_ref, o_ref):
    c = c_ref[...]
    s = c / (1.0 + jnp.exp(-c))
    o_ref[0] = _dot(s.astype(BF16), w_ref[0].astype(BF16)) + b_ref[0]


def _ada_modulation(cond, ada_w, ada_b):
    depth, d, n = ada_w.shape
    tn = 1536
    return pl.pallas_call(
        _ada_kernel,
        grid=(depth, n // tn),
        in_specs=[pl.BlockSpec((16, d), lambda l, j: (0, 0)),
                  pl.BlockSpec((1, d, tn), lambda l, j: (l, 0, j)),
                  pl.BlockSpec((1, 1, tn), lambda l, j: (l, 0, j))],
        out_specs=pl.BlockSpec((1, 16, tn), lambda l, j: (l, 0, j)),
        out_shape=jax.ShapeDtypeStruct((depth, 16, n), F32),
        compiler_params=_params(2),
        name="ada_modulation",
    )(cond, ada_w, ada_b.reshape(depth, 1, n))


def _rope_rows(blk, quarter, cf, sf):
    a, b = blk[0:quarter], blk[quarter:2 * quarter]
    c, d = blk[2 * quarter:3 * quarter], blk[3 * quarter:4 * quarter]
    cr, sr = cf[0:quarter], sf[0:quarter]
    cc, sc = cf[quarter:2 * quarter], sf[quarter:2 * quarter]
    return jnp.concatenate([a * cr - b * sr, b * cr + a * sr, c * cc - d * sc, d * cc + c * sc], axis=0)


def _rope_lanes(x, quarter, ct, s1, s2):
    return x * ct + pltpu.roll(x, 128 - quarter, 1) * s1 + pltpu.roll(x, quarter, 1) * s2


def _store_with_ones(vt_ref, tile, heads, dv):
    row = lax.broadcasted_iota(jnp.int32, (ONES_ROWS, tile.shape[1]), 0)
    ones_blk = jnp.where(row == 0, 1.0, 0.0).astype(BF16)
    aug = dv + ONES_ROWS
    for h in range(heads):
        vt_ref[aug * h:aug * h + dv, :] = tile[dv * h:dv * (h + 1)]
        vt_ref[aug * h + dv:aug * (h + 1), :] = ones_blk


def _ab_proj_kernel(x_ref, mod_ref, g_ref, wqt_ref, wk_ref, wvt_ref, cf_ref, sf_ref, ct_ref, s1_ref, s2_ref,
                    qt_ref, ka_ref, kb_ref, vta_ref, vtb_ref):
    h = _rms(x_ref[0], g_ref[...]) * (1.0 + mod_ref[0, 1:2, :]) + mod_ref[0, 0:1, :]
    hb = h.astype(BF16)
    cf, sf = cf_ref[...], sf_ref[...]
    qt = _dot_nt(wqt_ref[...], hb)
    for g in range(AB_Q_W // 64):
        blk = _rope_rows(qt[64 * g:64 * g + 64], 16, cf, sf)
        qt_ref[0, 64 * g:64 * g + 64, :] = (blk * (LOG2E * 64 ** -0.5)).astype(BF16)
    kk = _dot(hb, wk_ref[...])
    ct, s1, s2 = ct_ref[...], s1_ref[...], s2_ref[...]
    ka_ref[0] = _rope_lanes(kk[:, :A_KV_W], 16, ct, s1, s2).astype(BF16)
    for c in range(B_QK_W // 128):
        lo = A_KV_W + 128 * c
        kb_ref[0, :, 128 * c:128 * c + 128] = _rope_lanes(kk[:, lo:lo + 128], 16, ct, s1, s2).astype(BF16)
    vt = _dot_nt(wvt_ref[...], hb).astype(BF16)
    tka, tkb = vta_ref.shape[3], vtb_ref.shape[3]
    for j in range(vt.shape[1] // tka):
        _store_with_ones(vta_ref.at[0, j], vt[:A_KV_W, tka * j:tka * (j + 1)], A_KV_HEADS, A_HEAD_DIM)
    for j in range(vt.shape[1] // tkb):
        _store_with_ones(vtb_ref.at[0, j], vt[A_KV_W:, tkb * j:tkb * (j + 1)], B_HEADS, B_V_DIM)


def _ab_project(x, mod, g, wqt, wk, wvt, tabs, tm):
    bsz, rows, d = x.shape
    cf, sf, ct, s1, s2 = tabs
    mod_map = (lambda b, i: (b, 0, 0)) if mod.shape[0] == bsz else (lambda b, i: (0, 0, 0))
    tkb = min(tm, LAT_KEY_TILE)
    return pl.pallas_call(
        _ab_proj_kernel,
        grid=(bsz, rows // tm),
        in_specs=[pl.BlockSpec((1, tm, d), lambda b, i: (b, i, 0)),
                  pl.BlockSpec((1, 6, d), mod_map),
                  _const_spec((1, d)), _const_spec(wqt.shape), _const_spec(wk.shape), _const_spec(wvt.shape),
                  pl.BlockSpec((32, tm), lambda b, i: (0, i)),
                  pl.BlockSpec((32, tm), lambda b, i: (0, i)),
                  pl.BlockSpec((tm, 128), lambda b, i: (i, 0)),
                  pl.BlockSpec((tm, 128), lambda b, i: (i, 0)),
                  pl.BlockSpec((tm, 128), lambda b, i: (i, 0))],
        out_specs=[pl.BlockSpec((1, AB_Q_W, tm), lambda b, i: (b, 0, i)),
                   pl.BlockSpec((1, tm, A_KV_W), lambda b, i: (b, i, 0)),
                   pl.BlockSpec((1, tm, B_QK_W), lambda b, i: (b, i, 0)),
                   pl.BlockSpec((1, tm // KEY_TILE, A_KV_HEADS * A_V_AUG, KEY_TILE), lambda b, i: (b, i, 0, 0)),
                   pl.BlockSpec((1, tm // tkb, B_HEADS * B_V_AUG, tkb), lambda b, i: (b, i, 0, 0))],
        out_shape=[jax.ShapeDtypeStruct((bsz, AB_Q_W, rows), BF16),
                   jax.ShapeDtypeStruct((bsz, rows, A_KV_W), BF16),
                   jax.ShapeDtypeStruct((bsz, rows, B_QK_W), BF16),
                   jax.ShapeDtypeStruct((bsz, rows // KEY_TILE, A_KV_HEADS * A_V_AUG, KEY_TILE), BF16),
                   jax.ShapeDtypeStruct((bsz, rows // tkb, B_HEADS * B_V_AUG, tkb), BF16)],
        compiler_params=_params(2),
        name="ab_project",
    )(x, mod, g, wqt, wk, wvt, cf, sf, ct, s1, s2)


def _tile_start(t, size):
    return t * size if isinstance(t, int) else pl.multiple_of(t * size, size)


def _normalise(acc, dv):
    return acc[:dv] * (1.0 / acc[dv:dv + 1])


class _KeyTile(NamedTuple):
    k_ref: object
    v_ref: object
    index: object
    offset: int
    size: int
    valid: object = None


def _key_sequence(k_parts, v_parts):
    if len(k_parts) == 1:
        assert v_parts[0].shape[1] == 1
        return [_KeyTile(k_parts[0], v_parts[0], 0, 0, v_parts[0].shape[3])], None
    (k_lat, k_ctx), (v_lat, v_ctx) = k_parts, v_parts
    n_lat, tk = v_lat.shape[1], v_lat.shape[3]
    assert v_ctx.shape[1] == 1 and n_lat >= 3 and tk % 256 == 0
    halves = lambda i: [_KeyTile(k_lat, v_lat, i, 0, tk // 2), _KeyTile(k_lat, v_lat, i, tk // 2, tk // 2)]
    tiles = (halves(0) + [_KeyTile(k_lat, v_lat, i, 0, tk) for i in range(1, n_lat - 1)] + halves(n_lat - 1)
             + [_KeyTile(k_ctx, v_ctx, 0, 0, v_ctx.shape[3])])
    return tiles, (2, n_lat)


class _Segment(NamedTuple):
    chains: list
    tiles: list
    acc_ref: object
    uniform: object = None
    m0: object = None


def _attend_pipelined(segments, s_ref, dv):
    n = len(segments[0].chains)
    tq = segments[0].chains[0][0].shape[1]
    bases = [sum(len(seg.tiles) for seg in segments[:g]) for g in range(len(segments))]
    n_pos = bases[-1] + len(segments[-1].tiles)
    assert s_ref.shape[0] == PIPE_SLOTS and all(seg.acc_ref.shape[1] > dv for seg in segments)
    running = []
    for seg in segments:
        if seg.m0 is None:
            seg.acc_ref[...] = jnp.zeros(seg.acc_ref.shape, F32)
            running.append([jnp.full((1, tq), NEG_INF, F32) for _ in range(n)])
        else:
            row = lax.broadcasted_iota(jnp.int32, seg.acc_ref.shape, 1)
            seg.acc_ref[...] = jnp.where(row == dv, 1.0, 0.0)
            running.append(list(seg.m0))

    def locate(g, p):
        seg = segments[g]
        if isinstance(p, int):
            return seg.tiles[p]
        lo = seg.uniform[0]
        return seg.tiles[lo]._replace(index=p - (lo - seg.tiles[lo].index))

    def split(position):
        g = max(g for g in range(len(segments)) if bases[g] <= position)
        return g, position - bases[g]

    def park(g, p, slot, c):
        tile = locate(g, p)
        q_pad, lanes, _ = segments[g].chains[c]
        start = _tile_start(tile.index, tile.v_ref.shape[3]) + tile.offset
        s = _dot(tile.k_ref[0, pl.ds(start, tile.size), lanes], q_pad)
        if tile.valid is not None:
            s = jnp.where(tile.valid, s, NEG_INF)
        s_ref[slot, c, 0:tile.size, :] = s
        return jnp.max(s, axis=0, keepdims=True)

    def finish(g, p, slot, c, m, tile_max):
        tile, acc_ref = locate(g, p), segments[g].acc_ref
        m_new = jnp.maximum(m, tile_max)
        prob = jnp.exp2(s_ref[slot, c, 0:tile.size, :] - m_new)
        v_tile = tile.v_ref[0, tile.index, segments[g].chains[c][2], tile.offset:tile.offset + tile.size]
        acc_ref[c] = jnp.exp2(m - m_new) * acc_ref[c] + _dot(v_tile, prob.astype(BF16))
        return m_new

    def sweep(g, p, slot, ms, maxima, ahead):
        ahead_slot = (slot + PIPE_AHEAD) % PIPE_SLOTS
        for c in range(n):
            if ahead is not None:
                maxima[ahead_slot][c] = park(ahead[0], ahead[1], ahead_slot, c)
            ms[c] = finish(g, p, slot, c, ms[c], maxima[slot][c])

    def static_sweeps(g, lo, hi, ms, maxima):
        for p in range(lo, hi):
            position = bases[g] + p
            ahead = split(position + PIPE_AHEAD) if position + PIPE_AHEAD < n_pos else None
            sweep(g, p, position % PIPE_SLOTS, ms, maxima, ahead)

    def unpack(carry):
        return list(carry[:n]), [list(carry[n * (1 + s):n * (2 + s)]) for s in range(PIPE_SLOTS)]

    per_iter = PIPE_SLOTS * PIPE_GROUPS
    maxima = [[jnp.full((1, tq), NEG_INF, F32) for _ in range(n)] for _ in range(PIPE_SLOTS)]
    for position in range(min(PIPE_AHEAD, n_pos)):
        maxima[position] = [park(*split(position), position, c) for c in range(n)]
    for g, seg in enumerate(segments):
        loop_lo, n_iters = 0, 0
        if seg.uniform is not None:
            loop_lo, n_iters = seg.uniform[0], (seg.uniform[1] - seg.uniform[0] - PIPE_AHEAD) // per_iter

        def group(j, carry, g=g, loop_lo=loop_lo):
            ms, maxima = unpack(carry)
            for u in range(per_iter):
                p = loop_lo + per_iter * j + u
                sweep(g, p, (bases[g] + loop_lo + u) % PIPE_SLOTS, ms, maxima, (g, p + PIPE_AHEAD))
            return tuple(ms + sum(maxima, []))

        ms = running[g]
        static_sweeps(g, 0, loop_lo, ms, maxima)
        if n_iters:
            ms, maxima = unpack(lax.fori_loop(0, n_iters, group, tuple(ms + sum(maxima, []))))
        static_sweeps(g, loop_lo + per_iter * n_iters, len(seg.tiles), ms, maxima)
    return [[_normalise(seg.acc_ref[c], dv) for c in range(n)] for seg in segments]


def _split_parts(refs, n_parts):
    return refs[:n_parts], refs[n_parts:2 * n_parts], refs[2 * n_parts:]


B_HEADS_PER_STEP = 2


def _diff_attn_kernel(lam_init, n_parts, qt_ref, *refs):
    k_parts, v_parts, (lam_ref, g_ref, o_ref, acc_ref, s_ref) = _split_parts(refs, n_parts)
    n_seg = acc_ref.shape[0]
    tq = qt_ref.shape[2] // n_seg
    tiles, uniform = _key_sequence(k_parts, v_parts)
    segments = []
    for g in range(n_seg):
        chains = []
        for j in range(B_HEADS_PER_STEP):
            q = qt_ref[0, 128 * j:128 * (j + 1), tq * g:tq * (g + 1)]
            row = lax.broadcasted_iota(jnp.int32, q.shape, 0)
            zero = jnp.zeros_like(q)
            slab, vrows = slice(128 * j, 128 * (j + 1)), slice(B_V_AUG * j, B_V_AUG * (j + 1))
            chains.append((jnp.where(row < B_QK_DIM, q, zero), slab, vrows))
            chains.append((jnp.where(row >= B_QK_DIM, q, zero), slab, vrows))
        segments.append(_Segment(chains, tiles, acc_ref.at[g], uniform))
    outs = _attend_pipelined(segments, s_ref, B_V_DIM)
    lv = lam_ref[...]
    lam = (jnp.exp(jnp.sum(lv[0:1] * lv[1:2], axis=1, keepdims=True))
           - jnp.exp(jnp.sum(lv[2:3] * lv[3:4], axis=1, keepdims=True)) + lam_init)
    for g in range(n_seg):
        for j in range(B_HEADS_PER_STEP):
            dlt = outs[g][2 * j] - lam * outs[g][2 * j + 1]
            y = (dlt * lax.rsqrt(jnp.mean(dlt * dlt, axis=0, keepdims=True) + NORM_EPS) * g_ref[...]
                 * (1.0 - lam_init))
            o_ref[0, tq * g:tq * (g + 1), B_V_DIM * j:B_V_DIM * (j + 1)] = y.T.astype(BF16)


def _query_tiles_per_step(rows, tq):
    return Q_TILES_PER_STEP if rows % (Q_TILES_PER_STEP * tq) == 0 else 1


def _diff_attention(qt, k_parts, v_parts, lam_vec, subln_g, lam_init, tq):
    bsz, _, rows = qt.shape
    per = B_HEADS_PER_STEP
    n_seg = _query_tiles_per_step(rows, tq)
    q_off = A_Q_W // (128 * per)
    in_specs = [pl.BlockSpec((1, 128 * per, tq * n_seg), lambda b, h, i: (b, q_off + h, i))]
    for k in k_parts:
        in_specs.append(pl.BlockSpec((1, k.shape[1], 128 * per), lambda b, h, i: (b, 0, h)))
    for v in v_parts:
        in_specs.append(pl.BlockSpec((1, v.shape[1], B_V_AUG * per, v.shape[3]), lambda b, h, i: (b, 0, h, 0)))
    in_specs += [pl.BlockSpec(lam_vec.shape, lambda b, h, i: (0, 0)),
                 pl.BlockSpec((B_V_DIM, 1), lambda b, h, i: (0, 0))]
    return pl.pallas_call(
        functools.partial(_diff_attn_kernel, lam_init, len(k_parts)),
        grid=(bsz, B_HEADS // per, rows // (tq * n_seg)),
        in_specs=in_specs,
        out_specs=pl.BlockSpec((1, tq * n_seg, B_V_DIM * per), lambda b, h, i: (b, i, h)),
        out_shape=jax.ShapeDtypeStruct((bsz, rows, B_V_W), BF16),
        scratch_shapes=[pltpu.VMEM((n_seg, 2 * per, B_V_AUG, tq), F32),
                        pltpu.VMEM((PIPE_SLOTS, 2 * per, v_parts[0].shape[3], tq), F32)],
        compiler_params=_params(3),
        name="diff_attention",
    )(qt, *k_parts, *v_parts, lam_vec, subln_g.reshape(B_V_DIM, 1))


def _window_attn_kernel(windowed, n_parts, sink_ref, qt_ref, *refs):
    k_parts, v_parts, (o_ref, acc_ref, slab_ref, s_ref) = _split_parts(refs, n_parts)
    tq = qt_ref.shape[2]
    i = pl.program_id(1)
    group = A_HEADS // A_KV_HEADS
    zeros = jnp.zeros((A_HEAD_DIM, tq), BF16)
    if not windowed:
        tiles, _ = _key_sequence(k_parts, v_parts)
    else:
        assert 2 * WINDOW == KEY_TILE and tq % KEY_TILE == 0
        per_q, n_tiles = tq // KEY_TILE, v_parts[0].shape[1]
        first = i * per_q - 1
        rel = (lax.broadcasted_iota(jnp.int32, (KEY_TILE, tq), 1)
               - lax.broadcasted_iota(jnp.int32, (KEY_TILE, tq), 0))

        def window_tile(u, offset, size):
            t = first + u
            reach = jnp.where((t >= 0) & (t < n_tiles), WINDOW, -1)
            valid = jnp.abs(rel[:size] + (i * tq - t * KEY_TILE - offset)) <= reach
            return _KeyTile(k_parts[0], v_parts[0], jnp.clip(t, 0, n_tiles - 1), offset, size, valid)

        tiles = ([window_tile(0, WINDOW, WINDOW)] + [window_tile(u, 0, KEY_TILE) for u in range(1, per_q + 1)]
                 + [window_tile(per_q + 1, 0, WINDOW), _KeyTile(k_parts[1], v_parts[1], 0, 0, KEY_TILE)])
    for kv in range(A_KV_HEADS):
        rows = slice(A_V_AUG * kv, A_V_AUG * (kv + 1))
        chains, m0 = [], []
        for hq in range(group * kv, group * (kv + 1)):
            qh = qt_ref[0, A_HEAD_DIM * hq:A_HEAD_DIM * (hq + 1), :]
            chains.append((jnp.concatenate([qh, zeros] if kv == 0 else [zeros, qh], axis=0), slice(None), rows))
            m0.append(jnp.full((1, tq), sink_ref[hq] * LOG2E, F32))
        outs, = _attend_pipelined([_Segment(chains, tiles, acc_ref.at[kv], None, m0)], s_ref.at[kv], A_HEAD_DIM)
        for c in range(group):
            slab_ref[kv, A_HEAD_DIM * c:A_HEAD_DIM * (c + 1), :] = outs[c]
        width = group * A_HEAD_DIM
        o_ref[0, :, width * kv:width * (kv + 1)] = slab_ref[kv].T.astype(BF16)


def _window_attention(qt, k_parts, v_parts, sink, windowed, tq):
    bsz, _, rows = qt.shape
    in_specs = [pl.BlockSpec(memory_space=pltpu.SMEM),
                pl.BlockSpec((1, A_Q_W, tq), lambda b, i: (b, 0, i))]
    for k in k_parts:
        in_specs.append(pl.BlockSpec((1, k.shape[1], 128), lambda b, i: (b, 0, 0)))
    for v in v_parts:
        in_specs.append(pl.BlockSpec((1, v.shape[1], A_KV_HEADS * A_V_AUG, KEY_TILE), lambda b, i: (b, 0, 0, 0)))
    group = A_HEADS // A_KV_HEADS
    return pl.pallas_call(
        functools.partial(_window_attn_kernel, windowed, len(k_parts)),
        grid=(bsz, rows // tq),
        in_specs=in_specs,
        out_specs=pl.BlockSpec((1, tq, A_Q_W), lambda b, i: (b, i, 0)),
        out_shape=jax.ShapeDtypeStruct((bsz, rows, A_Q_W), BF16),
        scratch_shapes=[pltpu.VMEM((A_KV_HEADS, group, A_V_AUG, tq), F32),
                        pltpu.VMEM((A_KV_HEADS, group * A_HEAD_DIM, tq), F32),
                        pltpu.VMEM((A_KV_HEADS, PIPE_SLOTS, group, KEY_TILE, tq), F32)],
        compiler_params=_params(2),
        name="window_attention",
    )(sink, qt, *k_parts, *v_parts)


def _mla_proj_kernel(with_q, x_ref, mod_ref, g_ref, win_ref, wkr_ref, qg_ref, kvg_ref, wqt_ref, wkn_ref, wvt_ref,
                     cf_ref, sf_ref, ct_ref, s1_ref, s2_ref, *out_refs):
    h = _rms(x_ref[0], g_ref[...]) * (1.0 + mod_ref[0, 1:2, :]) + mod_ref[0, 0:1, :]
    hb = h.astype(BF16)
    lat = _dot(hb, win_ref[...])
    kvn = _rms(lat[:, C_Q_LORA:], kvg_ref[...]).astype(BF16)
    if with_q:
        qt_ref, k_ref, vt_ref = out_refs
        qn = _rms(lat[:, :C_Q_LORA], qg_ref[...]).astype(BF16)
        qt = _dot_nt(wqt_ref[...], qn)
        cf, sf = cf_ref[...], sf_ref[...]
        scale = LOG2E * (C_NOPE + C_ROPE) ** -0.5
        for hd in range(C_HEADS):
            base = C_SLAB * hd
            rope = _rope_rows(qt[base + C_NOPE:base + C_NOPE + C_ROPE], C_ROPE // 4, cf, sf)
            blk = jnp.concatenate([qt[base:base + C_NOPE], rope, qt[base + C_NOPE + C_ROPE:base + C_SLAB]], axis=0)
            qt_ref[0, base:base + C_SLAB, :] = (blk * scale).astype(BF16)
    else:
        k_ref, vt_ref = out_refs
    kn = _dot(kvn, wkn_ref[...])
    kr = _rope_lanes(_dot(hb, wkr_ref[...]), C_ROPE // 4, ct_ref[...], s1_ref[...], s2_ref[...])
    for hd in range(C_HEADS):
        k_ref[0, :, C_SLAB * hd:C_SLAB * (hd + 1)] = (kn[:, C_SLAB * hd:C_SLAB * (hd + 1)] + kr).astype(BF16)
    vt = _dot_nt(wvt_ref[...], kvn).astype(BF16)
    tk = vt_ref.shape[3]
    for j in range(vt.shape[1] // tk):
        _store_with_ones(vt_ref.at[0, j], vt[:, tk * j:tk * (j + 1)], C_HEADS, C_V)


def _mla_project(x, mod, g, weights, tabs, with_q, tm):
    bsz, rows, d = x.shape
    win, wkr, qg, kvg, wqt, wkn, wvt = weights
    cf, sf, ct, s1, s2 = tabs
    mod_map = (lambda b, i: (b, 0, 0)) if mod.shape[0] == bsz else (lambda b, i: (0, 0, 0))
    kw, vw, tk = C_HEADS * C_SLAB, C_HEADS * C_V_AUG, min(tm, LAT_KEY_TILE)
    out_specs = [pl.BlockSpec((1, tm, kw), lambda b, i: (b, i, 0)),
                 pl.BlockSpec((1, tm // tk, vw, tk), lambda b, i: (b, i, 0, 0))]
    out_shape = [jax.ShapeDtypeStruct((bsz, rows, kw), BF16),
                 jax.ShapeDtypeStruct((bsz, rows // tk, vw, tk), BF16)]
    if with_q:
        out_specs.insert(0, pl.BlockSpec((1, kw, tm), lambda b, i: (b, 0, i)))
        out_shape.insert(0, jax.ShapeDtypeStruct((bsz, kw, rows), BF16))
    return pl.pallas_call(
        functools.partial(_mla_proj_kernel, with_q),
        grid=(bsz, rows // tm),
        in_specs=[pl.BlockSpec((1, tm, d), lambda b, i: (b, i, 0)),
                  pl.BlockSpec((1, 6, d), mod_map),
                  _const_spec((1, d)), _const_spec(win.shape), _const_spec(wkr.shape),
                  _const_spec(qg.shape), _const_spec(kvg.shape),
                  _const_spec(wqt.shape), _const_spec(wkn.shape), _const_spec(wvt.shape),
                  pl.BlockSpec((16, tm), lambda b, i: (0, i)),
                  pl.BlockSpec((16, tm), lambda b, i: (0, i)),
                  pl.BlockSpec((tm, 128), lambda b, i: (i, 0)),
                  pl.BlockSpec((tm, 128), lambda b, i: (i, 0)),
                  pl.BlockSpec((tm, 128), lambda b, i: (i, 0))],
        out_specs=out_specs,
        out_shape=out_shape,
        compiler_params=_params(2),
        name="mla_project",
    )(x, mod, g, win, wkr, qg, kvg, wqt, wkn, wvt, cf, sf, ct, s1, s2)


C_HEADS_PER_STEP = 4


def _mla_attn_kernel(n_parts, qt_ref, *refs):
    k_parts, v_parts, (o_ref, acc_ref, slab_ref, s_ref) = _split_parts(refs, n_parts)
    n_seg = acc_ref.shape[0]
    tq = qt_ref.shape[2] // n_seg
    tiles, uniform = _key_sequence(k_parts, v_parts)
    segments = []
    for g in range(n_seg):
        chains = [(qt_ref[0, C_SLAB * j:C_SLAB * (j + 1), tq * g:tq * (g + 1)], slice(C_SLAB * j, C_SLAB * (j + 1)),
                   slice(C_V_AUG * j, C_V_AUG * (j + 1))) for j in range(C_HEADS_PER_STEP)]
        segments.append(_Segment(chains, tiles, acc_ref.at[g], uniform))
    outs = _attend_pipelined(segments, s_ref, C_V)
    for g in range(n_seg):
        for j in range(C_HEADS_PER_STEP):
            slab_ref[g, C_V * j:C_V * (j + 1), :] = outs[g][j]
        o_ref[0, tq * g:tq * (g + 1), :] = slab_ref[g].T.astype(BF16)


def _mla_attention(qt, k_parts, v_parts, tq):
    bsz, _, rows = qt.shape
    per = C_HEADS_PER_STEP
    n_seg = _query_tiles_per_step(rows, tq)
    in_specs = [pl.BlockSpec((1, per * C_SLAB, tq * n_seg), lambda b, p, i: (b, p, i))]
    for k in k_parts:
        in_specs.append(pl.BlockSpec((1, k.shape[1], per * C_SLAB), lambda b, p, i: (b, 0, p)))
    for v in v_parts:
        in_specs.append(pl.BlockSpec((1, v.shape[1], per * C_V_AUG, v.shape[3]), lambda b, p, i: (b, 0, p, 0)))
    return pl.pallas_call(
        functools.partial(_mla_attn_kernel, len(k_parts)),
        grid=(bsz, C_HEADS // per, rows // (tq * n_seg)),
        in_specs=in_specs,
        out_specs=pl.BlockSpec((1, tq * n_seg, per * C_V), lambda b, p, i: (b, i, p)),
        out_shape=jax.ShapeDtypeStruct((bsz, rows, C_HEADS * C_V), BF16),
        scratch_shapes=[pltpu.VMEM((n_seg, per, C_V_AUG, tq), F32), pltpu.VMEM((n_seg, per * C_V, tq), F32),
                        pltpu.VMEM((PIPE_SLOTS, per, v_parts[0].shape[3], tq), F32)],
        compiler_params=_params(3),
        name="mla_attention",
    )(qt, *k_parts, *v_parts)


def _mix_ffn_kernel(n_o, *refs):
    o_refs = refs[:n_o]
    x_ref, mod_ref, g_ref, wout_ref, w13_ref, w2_ref, out_ref = refs[n_o:]
    y = None
    row = 0
    for o_ref in o_refs:
        width = o_ref.shape[2]
        part = _dot(o_ref[0], wout_ref[row:row + width, :])
        y = part if y is None else y + part
        row += width
    x1 = x_ref[0] + mod_ref[0, 2:3, :] * _rms(y, g_ref[1:2, :])
    h = _rms(x1, g_ref[2:3, :]) * (1.0 + mod_ref[0, 4:5, :]) + mod_ref[0, 3:4, :]
    hb = h.astype(BF16)
    f = None
    for c in range(FFN_HIDDEN // FFN_CHUNK):
        lo = FFN_CHUNK * c
        gate = _dot(hb, w13_ref[:, lo:lo + FFN_CHUNK])
        up = _dot(hb, w13_ref[:, FFN_HIDDEN + lo:FFN_HIDDEN + lo + FFN_CHUNK])
        act = (gate / (1.0 + jnp.exp(-gate)) * up).astype(BF16)
        part = _dot(act, w2_ref[lo:lo + FFN_CHUNK, :])
        f = part if f is None else f + part
    out_ref[0] = x1 + mod_ref[0, 5:6, :] * _rms(f, g_ref[3:4, :])


def _mix_ffn(o_parts, x, mod, g4, wout, w13, w2, tm):
    bsz, rows, d = x.shape
    mod_map = (lambda b, i: (b, 0, 0)) if mod.shape[0] == bsz else (lambda b, i: (0, 0, 0))
    in_specs = [pl.BlockSpec((1, tm, o.shape[2]), lambda b, i: (b, i, 0)) for o in o_parts]
    in_specs += [pl.BlockSpec((1, tm, d), lambda b, i: (b, i, 0)),
                 pl.BlockSpec((1, 6, d), mod_map),
                 _const_spec(g4.shape), _const_spec(wout.shape), _const_spec(w13.shape), _const_spec(w2.shape)]
    return pl.pallas_call(
        functools.partial(_mix_ffn_kernel, len(o_parts)),
        grid=(bsz, rows // tm),
        in_specs=in_specs,
        out_specs=pl.BlockSpec((1, tm, d), lambda b, i: (b, i, 0)),
        out_shape=jax.ShapeDtypeStruct((bsz, rows, d), F32),
        compiler_params=_params(2),
        name="mix_ffn",
    )(*o_parts, x, mod, g4, wout, w13, w2)


def _rope_tables(n, rot_dim):
    pos = jnp.arange(n, dtype=jnp.int32)
    row = (pos // GRID_W).astype(F32)
    col = (pos % GRID_W).astype(F32)
    axis_dim = rot_dim // 2
    inv_freq = ROPE_THETA ** (-jnp.arange(0, axis_dim, 2, dtype=F32) / axis_dim)
    ang = jnp.concatenate([row[:, None] * inv_freq, col[:, None] * inv_freq], axis=-1)
    return jnp.cos(ang), jnp.sin(ang)


def _lane_tables(cos, sin, lane_dims, quarter):
    dims = jnp.asarray(lane_dims, dtype=jnp.int32)
    live = dims >= 0
    safe = jnp.where(live, dims, 0)
    axis, within = safe // (2 * quarter), safe % (2 * quarter)
    idx = axis * quarter + within % quarter
    first = within < quarter
    c = jnp.where(live[None, :], cos[:, idx], 1.0)
    s = jnp.where(live[None, :], sin[:, idx], 0.0)
    return c, jnp.where(first[None, :], -s, 0.0), jnp.where(first[None, :], 0.0, s)


def _identity_tables(n, half):
    return (jnp.ones((half, n), F32), jnp.zeros((half, n), F32),
            jnp.ones((n, 128), F32), jnp.zeros((n, 128), F32), jnp.zeros((n, 128), F32))


def kernel(x, c, ctx, c_ctx, ada_w, ada_b, norm_g, ffn_w13, ffn_w2, ab_w_in, ab_w_out, ab_sink, diff_lambda,
           diff_subln_g, mla_w_in, mla_q_norm_g, mla_kv_norm_g, mla_wq_b, mla_wkv_b, mla_w_out):
    bsz, n, d = x.shape
    n_ctx = ctx.shape[1]
    depth = ada_w.shape[0]
    assert n_ctx == KEY_TILE and n % Q_TILE == 0 and n % ROW_TILE == 0 and bsz + 1 <= 16

    cond = jnp.zeros((16, d), F32).at[:bsz].set(c).at[bsz].set(c_ctx)
    mods = _ada_modulation(cond, ada_w, ada_b).reshape(depth, 16, 6, d)

    cos_ab, sin_ab = _rope_tables(n, A_HEAD_DIM)
    ab_lane = [l % 64 for l in range(128)]
    tabs_ab = (cos_ab.T, sin_ab.T) + _lane_tables(cos_ab, sin_ab, ab_lane, 16)
    cos_c, sin_c = _rope_tables(n, C_ROPE)
    c_lane = [l - C_NOPE if C_NOPE <= l < C_NOPE + C_ROPE else -1 for l in range(128)]
    tabs_c = (cos_c.T, sin_c.T) + _lane_tables(cos_c, sin_c, c_lane, C_ROPE // 4)

    for l in range(depth):
        last = l == depth - 1
        mod_x, mod_c = mods[l, :bsz], mods[l, bsz:bsz + 1]
        g_in = norm_g[l, 0:1]
        w13, w2 = ffn_w13[l].astype(BF16), ffn_w2[l].astype(BF16)
        if l % 2 == 0:
            e = l // 2
            w = ab_w_in[e]
            o1, o2 = AB_Q_W + A_KV_W, AB_Q_W + 2 * A_KV_W
            o3 = o2 + B_QK_W
            wqt = w[:, :AB_Q_W].T.astype(BF16)
            wk = jnp.concatenate([w[:, AB_Q_W:o1], w[:, o2:o3]], axis=1).astype(BF16)
            wvt = jnp.concatenate([w[:, o1:o2], w[:, o3:]], axis=1).T.astype(BF16)
            wout = ab_w_out[e].astype(BF16)
            lam_init = 0.8 - 0.6 * math.exp(-0.3 * l)
            qt_x, ka_x, kb_x, va_x, vb_x = _ab_project(x, mod_x, g_in, wqt, wk, wvt, tabs_ab, ROW_TILE)
            qt_c, ka_c, kb_c, va_c, vb_c = _ab_project(ctx, mod_c, g_in, wqt, wk, wvt,
                                                       _identity_tables(n_ctx, 32), n_ctx)
            lam_args = (diff_lambda[e], diff_subln_g[e], lam_init)
            oa = _window_attention(qt_x, [ka_x, ka_c], [va_x, va_c], ab_sink[e], True, Q_TILE)
            ob = _diff_attention(qt_x, [kb_x, kb_c], [vb_x, vb_c], *lam_args, Q_TILE)
            o_x = [oa, ob]
            if not last:
                oca = _window_attention(qt_c, [ka_c], [va_c], ab_sink[e], False, n_ctx)
                ocb = _diff_attention(qt_c, [kb_c], [vb_c], *lam_args, n_ctx)
                o_c = [oca, ocb]
        else:
            o = l // 2
            w = mla_w_in[o]
            win = w[:, :C_LAT_W].astype(BF16)
            wkr = jnp.zeros((d, C_SLAB), F32).at[:, C_NOPE:C_NOPE + C_ROPE].set(w[:, C_LAT_W:]).astype(BF16)
            wq = mla_wq_b[o].reshape(C_Q_LORA, C_HEADS, C_NOPE + C_ROPE)
            wq = jnp.pad(wq, ((0, 0), (0, 0), (0, C_SLAB - C_NOPE - C_ROPE)))
            wqt = wq.reshape(C_Q_LORA, C_HEADS * C_SLAB).T.astype(BF16)
            wkv = mla_wkv_b[o].reshape(C_KV_LORA, C_HEADS, C_NOPE + C_V)
            wkn = jnp.pad(wkv[:, :, :C_NOPE], ((0, 0), (0, 0), (0, C_SLAB - C_NOPE)))
            wkn = wkn.reshape(C_KV_LORA, C_HEADS * C_SLAB).astype(BF16)
            wvt = wkv[:, :, C_NOPE:].reshape(C_KV_LORA, C_HEADS * C_V).T.astype(BF16)
            weights = (win, wkr, mla_q_norm_g[o:o + 1], mla_kv_norm_g[o:o + 1], wqt, wkn, wvt)
            wout = mla_w_out[o].astype(BF16)
            qt_x, k_x, vt_x = _mla_project(x, mod_x, g_in, weights, tabs_c, True, ROW_TILE)
            ident = _identity_tables(n_ctx, 16)
            if last:
                k_c, vt_c = _mla_project(ctx, mod_c, g_in, weights, ident, False, n_ctx)
            else:
                qt_c, k_c, vt_c = _mla_project(ctx, mod_c, g_in, weights, ident, True, n_ctx)
            o_x = [_mla_attention(qt_x, [k_x, k_c], [vt_x, vt_c], Q_TILE)]
            if not last:
                o_c = [_mla_attention(qt_c, [k_c], [vt_c], n_ctx)]
        x = _mix_ffn(o_x, x, mod_x, norm_g[l], wout, w13, w2, ROW_TILE)
        if not last:
            ctx = _mix_ffn(o_c, ctx, mod_c, norm_g[l], wout, w13, w2, n_ctx)
    return x
```

```python
import functools
import math
from typing import NamedTuple

import jax
import jax.numpy as jnp
from jax import lax
from jax.experimental import pallas as pl
from jax.experimental.pallas import tpu as pltpu

F32 = jnp.float32
BF16 = jnp.bfloat16

D_MODEL = 1024
GRID_W = 64
ROPE_THETA = 10000.0
NORM_EPS = 1e-6
NEG_INF = -1e30
WINDOW = 128
LOG2E = 1.4426950408889634

A_HEADS, A_KV_HEADS, A_HEAD_DIM = 8, 2, 64
B_HEADS, B_QK_DIM, B_V_DIM = 4, 64, 128
A_Q_W = A_HEADS * A_HEAD_DIM
A_KV_W = A_KV_HEADS * A_HEAD_DIM
B_QK_W = B_HEADS * 2 * B_QK_DIM
B_V_W = B_HEADS * B_V_DIM
AB_Q_W = A_Q_W + B_QK_W
AB_K_W = A_KV_W + B_QK_W
AB_V_W = A_KV_W + B_V_W

C_HEADS, C_Q_LORA, C_KV_LORA, C_NOPE, C_ROPE, C_V = 16, 384, 256, 64, 32, 64
C_SLAB = 128
C_LAT_W = C_Q_LORA + C_KV_LORA

FFN_HIDDEN = 2816
FFN_CHUNK = 256

ONES_ROWS = 16
A_V_AUG = A_HEAD_DIM + ONES_ROWS
B_V_AUG = B_V_DIM + ONES_ROWS
C_V_AUG = C_V + ONES_ROWS

KEY_TILE = 256
LAT_KEY_TILE = 512
PIPE_AHEAD = 2
PIPE_SLOTS = PIPE_AHEAD + 1
PIPE_GROUPS = 2
Q_TILES_PER_STEP = 2
Q_TILE = 512
ROW_TILE = 512
VMEM_LIMIT = 56 * 1024 * 1024

_NT = (((1,), (1,)), ((), ()))


def _dot(a, b):
    return jnp.dot(a, b, preferred_element_type=F32)


def _dot_nt(a, b):
    return lax.dot_general(a, b, _NT, preferred_element_type=F32)


def _rms(x, g):
    return x * lax.rsqrt(jnp.mean(x * x, axis=-1, keepdims=True) + NORM_EPS) * g


def _const_spec(shape):
    zeros = (0,) * len(shape)
    return pl.BlockSpec(shape, lambda *_: zeros, pipeline_mode=pl.Buffered(1))


def _params(n_axes):
    return pltpu.CompilerParams(dimension_semantics=("arbitrary",) * n_axes,
                                vmem_limit_bytes=VMEM_LIMIT)


def _ada_kernel(c_ref, w_ref, b_ref, o_ref):
    c = c_ref[...]
    s = c / (1.0 + jnp.exp(-c))
    o_ref[0] = _dot(s.astype(BF16), w_ref[0].astype(BF16)) + b_ref[0]


def _ada_modulation(cond, ada_w, ada_b):
    depth, d, n = ada_w.shape
    tn = 1536
    return pl.pallas_call(
        _ada_kernel,
        grid=(depth, n // tn),
        in_specs=[pl.BlockSpec((16, d), lambda l, j: (0, 0)),
                  pl.BlockSpec((1, d, tn), lambda l, j: (l, 0, j)),
                  pl.BlockSpec((1, 1, tn), lambda l, j: (l, 0, j))],
        out_specs=pl.BlockSpec((1, 16, tn), lambda l, j: (l, 0, j)),
        out_shape=jax.ShapeDtypeStruct((depth, 16, n), F32),
        compiler_params=_params(2),
        name="ada_modulation",
    )(cond, ada_w, ada_b.reshape(depth, 1, n))


def _rope_rows(blk, quarter, cf, sf):
    a, b = blk[0:quarter], blk[quarter:2 * quarter]
    c, d = blk[2 * quarter:3 * quarter], blk[3 * quarter:4 * quarter]
    cr, sr = cf[0:quarter], sf[0:quarter]
    cc, sc = cf[quarter:2 * quarter], sf[quarter:2 * quarter]
    return jnp.concatenate([a * cr - b * sr, b * cr + a * sr, c * cc - d * sc, d * cc + c * sc], axis=0)


def _rope_lanes(x, quarter, ct, s1, s2):
    return x * ct + pltpu.roll(x, 128 - quarter, 1) * s1 + pltpu.roll(x, quarter, 1) * s2


def _store_with_ones(vt_ref, tile, heads, dv):
    row = lax.broadcasted_iota(jnp.int32, (ONES_ROWS, tile.shape[1]), 0)
    ones_blk = jnp.where(row == 0, 1.0, 0.0).astype(BF16)
    aug = dv + ONES_ROWS
    for h in range(heads):
        vt_ref[aug * h:aug * h + dv, :] = tile[dv * h:dv * (h + 1)]
        vt_ref[aug * h + dv:aug * (h + 1), :] = ones_blk


def _ab_proj_kernel(x_ref, mod_ref, g_ref, wqt_ref, wk_ref, wvt_ref, cf_ref, sf_ref, ct_ref, s1_ref, s2_ref,
                    qt_ref, ka_ref, kb_ref, vta_ref, vtb_ref):
    h = _rms(x_ref[0], g_ref[...]) * (1.0 + mod_ref[0, 1:2, :]) + mod_ref[0, 0:1, :]
    hb = h.astype(BF16)
    cf, sf = cf_ref[...], sf_ref[...]
    qt = _dot_nt(wqt_ref[...], hb)
    for g in range(AB_Q_W // 64):
        blk = _rope_rows(qt[64 * g:64 * g + 64], 16, cf, sf)
        qt_ref[0, 64 * g:64 * g + 64, :] = (blk * (LOG2E * 64 ** -0.5)).astype(BF16)
    kk = _dot(hb, wk_ref[...])
    ct, s1, s2 = ct_ref[...], s1_ref[...], s2_ref[...]
    ka_ref[0] = _rope_lanes(kk[:, :A_KV_W], 16, ct, s1, s2).astype(BF16)
    for c in range(B_QK_W // 128):
        lo = A_KV_W + 128 * c
        kb_ref[0, :, 128 * c:128 * c + 128] = _rope_lanes(kk[:, lo:lo + 128], 16, ct, s1, s2).astype(BF16)
    vt = _dot_nt(wvt_ref[...], hb).astype(BF16)
    tka, tkb = vta_ref.shape[3], vtb_ref.shape[3]
    for j in range(vt.shape[1] // tka):
        _store_with_ones(vta_ref.at[0, j], vt[:A_KV_W, tka * j:tka * (j + 1)], A_KV_HEADS, A_HEAD_DIM)
    for j in range(vt.shape[1] // tkb):
        _store_with_ones(vtb_ref.at[0, j], vt[A_KV_W:, tkb * j:tkb * (j + 1)], B_HEADS, B_V_DIM)


def _ab_project(x, mod, g, wqt, wk, wvt, tabs, tm):
    bsz, rows, d = x.shape
    cf, sf, ct, s1, s2 = tabs
    mod_map = (lambda b, i: (b, 0, 0)) if mod.shape[0] == bsz else (lambda b, i: (0, 0, 0))
    tkb = min(tm, LAT_KEY_TILE)
    return pl.pallas_call(
        _ab_proj_kernel,
        grid=(bsz, rows // tm),
        in_specs=[pl.BlockSpec((1, tm, d), lambda b, i: (b, i, 0)),
                  pl.BlockSpec((1, 6, d), mod_map),
                  _const_spec((1, d)), _const_spec(wqt.shape), _const_spec(wk.shape), _const_spec(wvt.shape),
                  pl.BlockSpec((32, tm), lambda b, i: (0, i)),
                  pl.BlockSpec((32, tm), lambda b, i: (0, i)),
                  pl.BlockSpec((tm, 128), lambda b, i: (i, 0)),
                  pl.BlockSpec((tm, 128), lambda b, i: (i, 0)),
                  pl.BlockSpec((tm, 128), lambda b, i: (i, 0))],
        out_specs=[pl.BlockSpec((1, AB_Q_W, tm), lambda b, i: (b, 0, i)),
                   pl.BlockSpec((1, tm, A_KV_W), lambda b, i: (b, i, 0)),
                   pl.BlockSpec((1, tm, B_QK_W), lambda b, i: (b, i, 0)),
                   pl.BlockSpec((1, tm // KEY_TILE, A_KV_HEADS * A_V_AUG, KEY_TILE), lambda b, i: (b, i, 0, 0)),
                   pl.BlockSpec((1, tm // tkb, B_HEADS * B_V_AUG, tkb), lambda b, i: (b, i, 0, 0))],
        out_shape=[jax.ShapeDtypeStruct((bsz, AB_Q_W, rows), BF16),
                   jax.ShapeDtypeStruct((bsz, rows, A_KV_W), BF16),
                   jax.ShapeDtypeStruct((bsz, rows, B_QK_W), BF16),
                   jax.ShapeDtypeStruct((bsz, rows // KEY_TILE, A_KV_HEADS * A_V_AUG, KEY_TILE), BF16),
                   jax.ShapeDtypeStruct((bsz, rows // tkb, B_HEADS * B_V_AUG, tkb), BF16)],
        compiler_params=_params(2),
        name="ab_project",
    )(x, mod, g, wqt, wk, wvt, cf, sf, ct, s1, s2)


def _tile_start(t, size):
    return t * size if isinstance(t, int) else pl.multiple_of(t * size, size)


def _normalise(acc, dv):
    return acc[:dv] * (1.0 / acc[dv:dv + 1])


class _KeyTile(NamedTuple):
    k_ref: object
    v_ref: object
    index: object
    offset: int
    size: int
    valid: object = None


def _key_sequence(k_parts, v_parts):
    if len(k_parts) == 1:
        assert v_parts[0].shape[1] == 1
        return [_KeyTile(k_parts[0], v_parts[0], 0, 0, v_parts[0].shape[3])], None
    (k_lat, k_ctx), (v_lat, v_ctx) = k_parts, v_parts
    n_lat, tk = v_lat.shape[1], v_lat.shape[3]
    assert v_ctx.shape[1] == 1 and n_lat >= 3 and tk % 256 == 0
    halves = lambda i: [_KeyTile(k_lat, v_lat, i, 0, tk // 2), _KeyTile(k_lat, v_lat, i, tk // 2, tk // 2)]
    tiles = (halves(0) + [_KeyTile(k_lat, v_lat, i, 0, tk) for i in range(1, n_lat - 1)] + halves(n_lat - 1)
             + [_KeyTile(k_ctx, v_ctx, 0, 0, v_ctx.shape[3])])
    return tiles, (2, n_lat)


class _Segment(NamedTuple):
    chains: list
    tiles: list
    acc_ref: object
    uniform: object = None
    m0: object = None


def _attend_pipelined(segments, s_ref, dv):
    n = len(segments[0].chains)
    tq = segments[0].chains[0][0].shape[1]
    bases = [sum(len(seg.tiles) for seg in segments[:g]) for g in range(len(segments))]
    n_pos = bases[-1] + len(segments[-1].tiles)
    assert s_ref.shape[0] == PIPE_SLOTS and all(seg.acc_ref.shape[1] > dv for seg in segments)
    running = []
    for seg in segments:
        if seg.m0 is None:
            seg.acc_ref[...] = jnp.zeros(seg.acc_ref.shape, F32)
            running.append([jnp.full((1, tq), NEG_INF, F32) for _ in range(n)])
        else:
            row = lax.broadcasted_iota(jnp.int32, seg.acc_ref.shape, 1)
            seg.acc_ref[...] = jnp.where(row == dv, 1.0, 0.0)
            running.append(list(seg.m0))

    def locate(g, p):
        seg = segments[g]
        if isinstance(p, int):
            return seg.tiles[p]
        lo = seg.uniform[0]
        return seg.tiles[lo]._replace(index=p - (lo - seg.tiles[lo].index))

    def split(position):
        g = max(g for g in range(len(segments)) if bases[g] <= position)
        return g, position - bases[g]

    def park(g, p, slot, c):
        tile = locate(g, p)
        q_pad, lanes, _ = segments[g].chains[c]
        start = _tile_start(tile.index, tile.v_ref.shape[3]) + tile.offset
        s = _dot(tile.k_ref[0, pl.ds(start, tile.size), lanes], q_pad)
        if tile.valid is not None:
            s = jnp.where(tile.valid, s, NEG_INF)
        s_ref[slot, c, 0:tile.size, :] = s
        return jnp.max(s, axis=0, keepdims=True)

    def finish(g, p, slot, c, m, tile_max):
        tile, acc_ref = locate(g, p), segments[g].acc_ref
        m_new = jnp.maximum(m, tile_max)
        prob = jnp.exp2(s_ref[slot, c, 0:tile.size, :] - m_new)
        v_tile = tile.v_ref[0, tile.index, segments[g].chains[c][2], tile.offset:tile.offset + tile.size]
        acc_ref[c] = jnp.exp2(m - m_new) * acc_ref[c] + _dot(v_tile, prob.astype(BF16))
        return m_new

    def sweep(g, p, slot, ms, maxima, ahead):
        ahead_slot = (slot + PIPE_AHEAD) % PIPE_SLOTS
        for c in range(n):
            if ahead is not None:
                maxima[ahead_slot][c] = park(ahead[0], ahead[1], ahead_slot, c)
            ms[c] = finish(g, p, slot, c, ms[c], maxima[slot][c])

    def static_sweeps(g, lo, hi, ms, maxima):
        for p in range(lo, hi):
            position = bases[g] + p
            ahead = split(position + PIPE_AHEAD) if position + PIPE_AHEAD < n_pos else None
            sweep(g, p, position % PIPE_SLOTS, ms, maxima, ahead)

    def unpack(carry):
        return list(carry[:n]), [list(carry[n * (1 + s):n * (2 + s)]) for s in range(PIPE_SLOTS)]

    per_iter = PIPE_SLOTS * PIPE_GROUPS
    maxima = [[jnp.full((1, tq), NEG_INF, F32) for _ in range(n)] for _ in range(PIPE_SLOTS)]
    for position in range(min(PIPE_AHEAD, n_pos)):
        maxima[position] = [park(*split(position), position, c) for c in range(n)]
    for g, seg in enumerate(segments):
        loop_lo, n_iters = 0, 0
        if seg.uniform is not None:
            loop_lo, n_iters = seg.uniform[0], (seg.uniform[1] - seg.uniform[0] - PIPE_AHEAD) // per_iter

        def group(j, carry, g=g, loop_lo=loop_lo):
            ms, maxima = unpack(carry)
            for u in range(per_iter):
                p = loop_lo + per_iter * j + u
                sweep(g, p, (bases[g] + loop_lo + u) % PIPE_SLOTS, ms, maxima, (g, p + PIPE_AHEAD))
            return tuple(ms + sum(maxima, []))

        ms = running[g]
        static_sweeps(g, 0, loop_lo, ms, maxima)
        if n_iters:
            ms, maxima = unpack(lax.fori_loop(0, n_iters, group, tuple(ms + sum(maxima, []))))
        static_sweeps(g, loop_lo + per_iter * n_iters, len(seg.tiles), ms, maxima)
    return [[_normalise(seg.acc_ref[c], dv) for c in range(n)] for seg in segments]


def _split_parts(refs, n_parts):
    return refs[:n_parts], refs[n_parts:2 * n_parts], refs[2 * n_parts:]


B_HEADS_PER_STEP = 2


def _diff_attn_kernel(lam_init, n_parts, qt_ref, *refs):
    k_parts, v_parts, (lam_ref, g_ref, o_ref, acc_ref, s_ref) = _split_parts(refs, n_parts)
    n_seg = acc_ref.shape[0]
    tq = qt_ref.shape[2] // n_seg
    tiles, uniform = _key_sequence(k_parts, v_parts)
    segments = []
    for g in range(n_seg):
        chains = []
        for j in range(B_HEADS_PER_STEP):
            q = qt_ref[0, 128 * j:128 * (j + 1), tq * g:tq * (g + 1)]
            row = lax.broadcasted_iota(jnp.int32, q.shape, 0)
            zero = jnp.zeros_like(q)
            slab, vrows = slice(128 * j, 128 * (j + 1)), slice(B_V_AUG * j, B_V_AUG * (j + 1))
            chains.append((jnp.where(row < B_QK_DIM, q, zero), slab, vrows))
            chains.append((jnp.where(row >= B_QK_DIM, q, zero), slab, vrows))
        segments.append(_Segment(chains, tiles, acc_ref.at[g], uniform))
    outs = _attend_pipelined(segments, s_ref, B_V_DIM)
    lv = lam_ref[...]
    lam = (jnp.exp(jnp.sum(lv[0:1] * lv[1:2], axis=1, keepdims=True))
           - jnp.exp(jnp.sum(lv[2:3] * lv[3:4], axis=1, keepdims=True)) + lam_init)
    for g in range(n_seg):
        for j in range(B_HEADS_PER_STEP):
            dlt = outs[g][2 * j] - lam * outs[g][2 * j + 1]
            y = (dlt * lax.rsqrt(jnp.mean(dlt * dlt, axis=0, keepdims=True) + NORM_EPS) * g_ref[...]
                 * (1.0 - lam_init))
            o_ref[0, tq * g:tq * (g + 1), B_V_DIM * j:B_V_DIM * (j + 1)] = y.T.astype(BF16)


def _query_tiles_per_step(rows, tq):
    return Q_TILES_PER_STEP if rows % (Q_TILES_PER_STEP * tq) == 0 else 1


def _diff_attention(qt, k_parts, v_parts, lam_vec, subln_g, lam_init, tq):
    bsz, _, rows = qt.shape
    per = B_HEADS_PER_STEP
    n_seg = _query_tiles_per_step(rows, tq)
    q_off = A_Q_W // (128 * per)
    in_specs = [pl.BlockSpec((1, 128 * per, tq * n_seg), lambda b, h, i: (b, q_off + h, i))]
    for k in k_parts:
        in_specs.append(pl.BlockSpec((1, k.shape[1], 128 * per), lambda b, h, i: (b, 0, h)))
    for v in v_parts:
        in_specs.append(pl.BlockSpec((1, v.shape[1], B_V_AUG * per, v.shape[3]), lambda b, h, i: (b, 0, h, 0)))
    in_specs += [pl.BlockSpec(lam_vec.shape, lambda b, h, i: (0, 0)),
                 pl.BlockSpec((B_V_DIM, 1), lambda b, h, i: (0, 0))]
    return pl.pallas_call(
        functools.partial(_diff_attn_kernel, lam_init, len(k_parts)),
        grid=(bsz, B_HEADS // per, rows // (tq * n_seg)),
        in_specs=in_specs,
        out_specs=pl.BlockSpec((1, tq * n_seg, B_V_DIM * per), lambda b, h, i: (b, i, h)),
        out_shape=jax.ShapeDtypeStruct((bsz, rows, B_V_W), BF16),
        scratch_shapes=[pltpu.VMEM((n_seg, 2 * per, B_V_AUG, tq), F32),
                        pltpu.VMEM((PIPE_SLOTS, 2 * per, v_parts[0].shape[3], tq), F32)],
        compiler_params=_params(3),
        name="diff_attention",
    )(qt, *k_parts, *v_parts, lam_vec, subln_g.reshape(B_V_DIM, 1))


def _window_attn_kernel(windowed, n_parts, sink_ref, qt_ref, *refs):
    k_parts, v_parts, (o_ref, acc_ref, slab_ref, s_ref) = _split_parts(refs, n_parts)
    tq = qt_ref.shape[2]
    i = pl.program_id(1)
    group = A_HEADS // A_KV_HEADS
    zeros = jnp.zeros((A_HEAD_DIM, tq), BF16)
    if not windowed:
        tiles, _ = _key_sequence(k_parts, v_parts)
    else:
        assert 2 * WINDOW == KEY_TILE and tq % KEY_TILE == 0
        per_q, n_tiles = tq // KEY_TILE, v_parts[0].shape[1]
        first = i * per_q - 1
        rel = (lax.broadcasted_iota(jnp.int32, (KEY_TILE, tq), 1)
               - lax.broadcasted_iota(jnp.int32, (KEY_TILE, tq), 0))

        def window_tile(u, offset, size):
            t = first + u
            reach = jnp.where((t >= 0) & (t < n_tiles), WINDOW, -1)
            valid = jnp.abs(rel[:size] + (i * tq - t * KEY_TILE - offset)) <= reach
            return _KeyTile(k_parts[0], v_parts[0], jnp.clip(t, 0, n_tiles - 1), offset, size, valid)

        tiles = ([window_tile(0, WINDOW, WINDOW)] + [window_tile(u, 0, KEY_TILE) for u in range(1, per_q + 1)]
                 + [window_tile(per_q + 1, 0, WINDOW), _KeyTile(k_parts[1], v_parts[1], 0, 0, KEY_TILE)])
    for kv in range(A_KV_HEADS):
        rows = slice(A_V_AUG * kv, A_V_AUG * (kv + 1))
        chains, m0 = [], []
        for hq in range(group * kv, group * (kv + 1)):
            qh = qt_ref[0, A_HEAD_DIM * hq:A_HEAD_DIM * (hq + 1), :]
            chains.append((jnp.concatenate([qh, zeros] if kv == 0 else [zeros, qh], axis=0), slice(None), rows))
            m0.append(jnp.full((1, tq), sink_ref[hq] * LOG2E, F32))
        outs, = _attend_pipelined([_Segment(chains, tiles, acc_ref.at[kv], None, m0)], s_ref.at[kv], A_HEAD_DIM)
        for c in range(group):
            slab_ref[kv, A_HEAD_DIM * c:A_HEAD_DIM * (c + 1), :] = outs[c]
        width = group * A_HEAD_DIM
        o_ref[0, :, width * kv:width * (kv + 1)] = slab_ref[kv].T.astype(BF16)


def _window_attention(qt, k_parts, v_parts, sink, windowed, tq):
    bsz, _, rows = qt.shape
    in_specs = [pl.BlockSpec(memory_space=pltpu.SMEM),
                pl.BlockSpec((1, A_Q_W, tq), lambda b, i: (b, 0, i))]
    for k in k_parts:
        in_specs.append(pl.BlockSpec((1, k.shape[1], 128), lambda b, i: (b, 0, 0)))
    for v in v_parts:
        in_specs.append(pl.BlockSpec((1, v.shape[1], A_KV_HEADS * A_V_AUG, KEY_TILE), lambda b, i: (b, 0, 0, 0)))
    group = A_HEADS // A_KV_HEADS
    return pl.pallas_call(
        functools.partial(_window_attn_kernel, windowed, len(k_parts)),
        grid=(bsz, rows // tq),
        in_specs=in_specs,
        out_specs=pl.BlockSpec((1, tq, A_Q_W), lambda b, i: (b, i, 0)),
        out_shape=jax.ShapeDtypeStruct((bsz, rows, A_Q_W), BF16),
        scratch_shapes=[pltpu.VMEM((A_KV_HEADS, group, A_V_AUG, tq), F32),
                        pltpu.VMEM((A_KV_HEADS, group * A_HEAD_DIM, tq), F32),
                        pltpu.VMEM((A_KV_HEADS, PIPE_SLOTS, group, KEY_TILE, tq), F32)],
        compiler_params=_params(2),
        name="window_attention",
    )(sink, qt, *k_parts, *v_parts)


def _mla_proj_kernel(with_q, x_ref, mod_ref, g_ref, win_ref, wkr_ref, qg_ref, kvg_ref, wqt_ref, wkn_ref, wvt_ref,
                     cf_ref, sf_ref, ct_ref, s1_ref, s2_ref, *out_refs):
    h = _rms(x_ref[0], g_ref[...]) * (1.0 + mod_ref[0, 1:2, :]) + mod_ref[0, 0:1, :]
    hb = h.astype(BF16)
    lat = _dot(hb, win_ref[...])
    kvn = _rms(lat[:, C_Q_LORA:], kvg_ref[...]).astype(BF16)
    if with_q:
        qt_ref, k_ref, vt_ref = out_refs
        qn = _rms(lat[:, :C_Q_LORA], qg_ref[...]).astype(BF16)
        qt = _dot_nt(wqt_ref[...], qn)
        cf, sf = cf_ref[...], sf_ref[...]
        scale = LOG2E * (C_NOPE + C_ROPE) ** -0.5
        for hd in range(C_HEADS):
            base = C_SLAB * hd
            rope = _rope_rows(qt[base + C_NOPE:base + C_NOPE + C_ROPE], C_ROPE // 4, cf, sf)
            blk = jnp.concatenate([qt[base:base + C_NOPE], rope, qt[base + C_NOPE + C_ROPE:base + C_SLAB]], axis=0)
            qt_ref[0, base:base + C_SLAB, :] = (blk * scale).astype(BF16)
    else:
        k_ref, vt_ref = out_refs
    kn = _dot(kvn, wkn_ref[...])
    kr = _rope_lanes(_dot(hb, wkr_ref[...]), C_ROPE // 4, ct_ref[...], s1_ref[...], s2_ref[...])
    for hd in range(C_HEADS):
        k_ref[0, :, C_SLAB * hd:C_SLAB * (hd + 1)] = (kn[:, C_SLAB * hd:C_SLAB * (hd + 1)] + kr).astype(BF16)
    vt = _dot_nt(wvt_ref[...], kvn).astype(BF16)
    tk = vt_ref.shape[3]
    for j in range(vt.shape[1] // tk):
        _store_with_ones(vt_ref.at[0, j], vt[:, tk * j:tk * (j + 1)], C_HEADS, C_V)


def _mla_project(x, mod, g, weights, tabs, with_q, tm):
    bsz, rows, d = x.shape
    win, wkr, qg, kvg, wqt, wkn, wvt = weights
    cf, sf, ct, s1, s2 = tabs
    mod_map = (lambda b, i: (b, 0, 0)) if mod.shape[0] == bsz else (lambda b, i: (0, 0, 0))
    kw, vw, tk = C_HEADS * C_SLAB, C_HEADS * C_V_AUG, min(tm, LAT_KEY_TILE)
    out_specs = [pl.BlockSpec((1, tm, kw), lambda b, i: (b, i, 0)),
                 pl.BlockSpec((1, tm // tk, vw, tk), lambda b, i: (b, i, 0, 0))]
    out_shape = [jax.ShapeDtypeStruct((bsz, rows, kw), BF16),
                 jax.ShapeDtypeStruct((bsz, rows // tk, vw, tk), BF16)]
    if with_q:
        out_specs.insert(0, pl.BlockSpec((1, kw, tm), lambda b, i: (b, 0, i)))
        out_shape.insert(0, jax.ShapeDtypeStruct((bsz, kw, rows), BF16))
    return pl.pallas_call(
        functools.partial(_mla_proj_kernel, with_q),
        grid=(bsz, rows // tm),
        in_specs=[pl.BlockSpec((1, tm, d), lambda b, i: (b, i, 0)),
                  pl.BlockSpec((1, 6, d), mod_map),
                  _const_spec((1, d)), _const_spec(win.shape), _const_spec(wkr.shape),
                  _const_spec(qg.shape), _const_spec(kvg.shape),
                  _const_spec(wqt.shape), _const_spec(wkn.shape), _const_spec(wvt.shape),
                  pl.BlockSpec((16, tm), lambda b, i: (0, i)),
                  pl.BlockSpec((16, tm), lambda b, i: (0, i)),
                  pl.BlockSpec((tm, 128), lambda b, i: (i, 0)),
                  pl.BlockSpec((tm, 128), lambda b, i: (i, 0)),
                  pl.BlockSpec((tm, 128), lambda b, i: (i, 0))],
        out_specs=out_specs,
        out_shape=out_shape,
        compiler_params=_params(2),
        name="mla_project",
    )(x, mod, g, win, wkr, qg, kvg, wqt, wkn, wvt, cf, sf, ct, s1, s2)


C_HEADS_PER_STEP = 4


def _mla_attn_kernel(n_parts, qt_ref, *refs):
    k_parts, v_parts, (o_ref, acc_ref, slab_ref, s_ref) = _split_parts(refs, n_parts)
    n_seg = acc_ref.shape[0]
    tq = qt_ref.shape[2] // n_seg
    tiles, uniform = _key_sequence(k_parts, v_parts)
    segments = []
    for g in range(n_seg):
        chains = [(qt_ref[0, C_SLAB * j:C_SLAB * (j + 1), tq * g:tq * (g + 1)], slice(C_SLAB * j, C_SLAB * (j + 1)),
                   slice(C_V_AUG * j, C_V_AUG * (j + 1))) for j in range(C_HEADS_PER_STEP)]
        segments.append(_Segment(chains, tiles, acc_ref.at[g], uniform))
    outs = _attend_pipelined(segments, s_ref, C_V)
    for g in range(n_seg):
        for j in range(C_HEADS_PER_STEP):
            slab_ref[g, C_V * j:C_V * (j + 1), :] = outs[g][j]
        o_ref[0, tq * g:tq * (g + 1), :] = slab_ref[g].T.astype(BF16)


def _mla_attention(qt, k_parts, v_parts, tq):
    bsz, _, rows = qt.shape
    per = C_HEADS_PER_STEP
    n_seg = _query_tiles_per_step(rows, tq)
    in_specs = [pl.BlockSpec((1, per * C_SLAB, tq * n_seg), lambda b, p, i: (b, p, i))]
    for k in k_parts:
        in_specs.append(pl.BlockSpec((1, k.shape[1], per * C_SLAB), lambda b, p, i: (b, 0, p)))
    for v in v_parts:
        in_specs.append(pl.BlockSpec((1, v.shape[1], per * C_V_AUG, v.shape[3]), lambda b, p, i: (b, 0, p, 0)))
    return pl.pallas_call(
        functools.partial(_mla_attn_kernel, len(k_parts)),
        grid=(bsz, C_HEADS // per, rows // (tq * n_seg)),
        in_specs=in_specs,
        out_specs=pl.BlockSpec((1, tq * n_seg, per * C_V), lambda b, p, i: (b, i, p)),
        out_shape=jax.ShapeDtypeStruct((bsz, rows, C_HEADS * C_V), BF16),
        scratch_shapes=[pltpu.VMEM((n_seg, per, C_V_AUG, tq), F32), pltpu.VMEM((n_seg, per * C_V, tq), F32),
                        pltpu.VMEM((PIPE_SLOTS, per, v_parts[0].shape[3], tq), F32)],
        compiler_params=_params(3),
        name="mla_attention",
    )(qt, *k_parts, *v_parts)


def _mix_ffn_kernel(n_o, *refs):
    o_refs = refs[:n_o]
    x_ref, mod_ref, g_ref, wout_ref, w13_ref, w2_ref, out_ref = refs[n_o:]
    y = None
    row = 0
    for o_ref in o_refs:
        width = o_ref.shape[2]
        part = _dot(o_ref[0], wout_ref[row:row + width, :])
        y = part if y is None else y + part
        row += width
    x1 = x_ref[0] + mod_ref[0, 2:3, :] * _rms(y, g_ref[1:2, :])
    h = _rms(x1, g_ref[2:3, :]) * (1.0 + mod_ref[0, 4:5, :]) + mod_ref[0, 3:4, :]
    hb = h.astype(BF16)
    f = None
    for c in range(FFN_HIDDEN // FFN_CHUNK):
        lo = FFN_CHUNK * c
        gate = _dot(hb, w13_ref[:, lo:lo + FFN_CHUNK])
        up = _dot(hb, w13_ref[:, FFN_HIDDEN + lo:FFN_HIDDEN + lo + FFN_CHUNK])
        act = (gate / (1.0 + jnp.exp(-gate)) * up).astype(BF16)
        part = _dot(act, w2_ref[lo:lo + FFN_CHUNK, :])
        f = part if f is None else f + part
    out_ref[0] = x1 + mod_ref[0, 5:6, :] * _rms(f, g_ref[3:4, :])


def _mix_ffn(o_parts, x, mod, g4, wout, w13, w2, tm):
    bsz, rows, d = x.shape
    mod_map = (lambda b, i: (b, 0, 0)) if mod.shape[0] == bsz else (lambda b, i: (0, 0, 0))
    in_specs = [pl.BlockSpec((1, tm, o.shape[2]), lambda b, i: (b, i, 0)) for o in o_parts]
    in_specs += [pl.BlockSpec((1, tm, d), lambda b, i: (b, i, 0)),
                 pl.BlockSpec((1, 6, d), mod_map),
                 _const_spec(g4.shape), _const_spec(wout.shape), _const_spec(w13.shape), _const_spec(w2.shape)]
    return pl.pallas_call(
        functools.partial(_mix_ffn_kernel, len(o_parts)),
        grid=(bsz, rows // tm),
        in_specs=in_specs,
        out_specs=pl.BlockSpec((1, tm, d), lambda b, i: (b, i, 0)),
        out_shape=jax.ShapeDtypeStruct((bsz, rows, d), F32),
        compiler_params=_params(2),
        name="mix_ffn",
    )(*o_parts, x, mod, g4, wout, w13, w2)


def _rope_tables(n, rot_dim):
    pos = jnp.arange(n, dtype=jnp.int32)
    row = (pos // GRID_W).astype(F32)
    col = (pos % GRID_W).astype(F32)
    axis_dim = rot_dim // 2
    inv_freq = ROPE_THETA ** (-jnp.arange(0, axis_dim, 2, dtype=F32) / axis_dim)
    ang = jnp.concatenate([row[:, None] * inv_freq, col[:, None] * inv_freq], axis=-1)
    return jnp.cos(ang), jnp.sin(ang)


def _lane_tables(cos, sin, lane_dims, quarter):
    dims = jnp.asarray(lane_dims, dtype=jnp.int32)
    live = dims >= 0
    safe = jnp.where(live, dims, 0)
    axis, within = safe // (2 * quarter), safe % (2 * quarter)
    idx = axis * quarter + within % quarter
    first = within < quarter
    c = jnp.where(live[None, :], cos[:, idx], 1.0)
    s = jnp.where(live[None, :], sin[:, idx], 0.0)
    return c, jnp.where(first[None, :], -s, 0.0), jnp.where(first[None, :], 0.0, s)


def _identity_tables(n, half):
    return (jnp.ones((half, n), F32), jnp.zeros((half, n), F32),
            jnp.ones((n, 128), F32), jnp.zeros((n, 128), F32), jnp.zeros((n, 128), F32))


def kernel(x, c, ctx, c_ctx, ada_w, ada_b, norm_g, ffn_w13, ffn_w2, ab_w_in, ab_w_out, ab_sink, diff_lambda,
           diff_subln_g, mla_w_in, mla_q_norm_g, mla_kv_norm_g, mla_wq_b, mla_wkv_b, mla_w_out):
    bsz, n, d = x.shape
    n_ctx = ctx.shape[1]
    depth = ada_w.shape[0]
    assert n_ctx == KEY_TILE and n % Q_TILE == 0 and n % ROW_TILE == 0 and bsz + 1 <= 16

    cond = jnp.zeros((16, d), F32).at[:bsz].set(c).at[bsz].set(c_ctx)
    mods = _ada_modulation(cond, ada_w, ada_b).reshape(depth, 16, 6, d)

    cos_ab, sin_ab = _rope_tables(n, A_HEAD_DIM)
    ab_lane = [l % 64 for l in range(128)]
    tabs_ab = (cos_ab.T, sin_ab.T) + _lane_tables(cos_ab, sin_ab, ab_lane, 16)
    cos_c, sin_c = _rope_tables(n, C_ROPE)
    c_lane = [l - C_NOPE if C_NOPE <= l < C_NOPE + C_ROPE else -1 for l in range(128)]
    tabs_c = (cos_c.T, sin_c.T) + _lane_tables(cos_c, sin_c, c_lane, C_ROPE // 4)

    for l in range(depth):
        last = l == depth - 1
        mod_x, mod_c = mods[l, :bsz], mods[l, bsz:bsz + 1]
        g_in = norm_g[l, 0:1]
        w13, w2 = ffn_w13[l].astype(BF16), ffn_w2[l].astype(BF16)
        if l % 2 == 0:
            e = l // 2
            w = ab_w_in[e]
            o1, o2 = AB_Q_W + A_KV_W, AB_Q_W + 2 * A_KV_W
            o3 = o2 + B_QK_W
            wqt = w[:, :AB_Q_W].T.astype(BF16)
            wk = jnp.concatenate([w[:, AB_Q_W:o1], w[:, o2:o3]], axis=1).astype(BF16)
            wvt = jnp.concatenate([w[:, o1:o2], w[:, o3:]], axis=1).T.astype(BF16)
            wout = ab_w_out[e].astype(BF16)
            lam_init = 0.8 - 0.6 * math.exp(-0.3 * l)
            qt_x, ka_x, kb_x, va_x, vb_x = _ab_project(x, mod_x, g_in, wqt, wk, wvt, tabs_ab, 2 * ROW_TILE)
            qt_c, ka_c, kb_c, va_c, vb_c = _ab_project(ctx, mod_c, g_in, wqt, wk, wvt,
                                                       _identity_tables(n_ctx, 32), n_ctx)
            lam_args = (diff_lambda[e], diff_subln_g[e], lam_init)
            oa = _window_attention(qt_x, [ka_x, ka_c], [va_x, va_c], ab_sink[e], True, Q_TILE)
            ob = _diff_attention(qt_x, [kb_x, kb_c], [vb_x, vb_c], *lam_args, Q_TILE)
            o_x = [oa, ob]
            if not last:
                oca = _window_attention(qt_c, [ka_c], [va_c], ab_sink[e], False, n_ctx)
                ocb = _diff_attention(qt_c, [kb_c], [vb_c], *lam_args, n_ctx)
                o_c = [oca, ocb]
        else:
            o = l // 2
            w = mla_w_in[o]
            win = w[:, :C_LAT_W].astype(BF16)
            wkr = jnp.zeros((d, C_SLAB), F32).at[:, C_NOPE:C_NOPE + C_ROPE].set(w[:, C_LAT_W:]).astype(BF16)
            wq = mla_wq_b[o].reshape(C_Q_LORA, C_HEADS, C_NOPE + C_ROPE)
            wq = jnp.pad(wq, ((0, 0), (0, 0), (0, C_SLAB - C_NOPE - C_ROPE)))
            wqt = wq.reshape(C_Q_LORA, C_HEADS * C_SLAB).T.astype(BF16)
            wkv = mla_wkv_b[o].reshape(C_KV_LORA, C_HEADS, C_NOPE + C_V)
            wkn = jnp.pad(wkv[:, :, :C_NOPE], ((0, 0), (0, 0), (0, C_SLAB - C_NOPE)))
            wkn = wkn.reshape(C_KV_LORA, C_HEADS * C_SLAB).astype(BF16)
            wvt = wkv[:, :, C_NOPE:].reshape(C_KV_LORA, C_HEADS * C_V).T.astype(BF16)
            weights = (win, wkr, mla_q_norm_g[o:o + 1], mla_kv_norm_g[o:o + 1], wqt, wkn, wvt)
            wout = mla_w_out[o].astype(BF16)
            qt_x, k_x, vt_x = _mla_project(x, mod_x, g_in, weights, tabs_c, True, ROW_TILE)
            ident = _identity_tables(n_ctx, 16)
            if last:
                k_c, vt_c = _mla_project(ctx, mod_c, g_in, weights, ident, False, n_ctx)
            else:
                qt_c, k_c, vt_c = _mla_project(ctx, mod_c, g_in, weights, ident, True, n_ctx)
            o_x = [_mla_attention(qt_x, [k_x, k_c], [vt_x, vt_c], Q_TILE)]
            if not last:
                o_c = [_mla_attention(qt_c, [k_c], [vt_c], n_ctx)]
        x = _mix_ffn(o_x, x, mod_x, norm_g[l], wout, w13, w2, ROW_TILE)
        if not last:
            ctx = _mix_ffn(o_c, ctx, mod_c, norm_g[l], wout, w13, w2, n_ctx)
    return x
```
